```python
import math
import jax, jax.numpy as jnp
from jax import lax
import numpy as np

D_MODEL = 1024
BATCH = 4
SEQ = 8192
DEPTH = 1

D_MIX = D_MODEL
D_ATTN = D_MIX // 2
D_SGU = D_MIX - D_ATTN
ATTN_HEADS = 4
ATTN_HEAD_DIM = D_ATTN // (2 * ATTN_HEADS)
ATTN_V_DIM = 2 * ATTN_HEAD_DIM
SGU_GROUPS = 4
SGU_GROUP_DIM = D_SGU // SGU_GROUPS
CHUNK = 128
Q_BLOCK = 128
D_IN = 3 * D_ATTN + 2 * D_SGU
D_FF = -(-8 * D_MODEL // (3 * 256)) * 256
DEEPNORM_ALPHA = (2 * DEPTH) ** 0.25
DEEPNORM_BETA = (8 * DEPTH) ** -0.25
LN_EPS = 1e-5
RMS_EPS = 1e-5

kernel_name = "hymba_diffattn_sgu_deepnorm_adaln"


def layer_norm(x, g, b):
    xf = x.astype(jnp.float32)
    mu = jnp.mean(xf, axis=-1, keepdims=True)
    var = jnp.mean(jnp.square(xf - mu), axis=-1, keepdims=True)
    y = (xf - mu) * lax.rsqrt(var + LN_EPS)
    return (y * g.astype(jnp.float32) + b.astype(jnp.float32)).astype(x.dtype)


def rms_norm(x, g):
    xf = x.astype(jnp.float32)
    y = xf * lax.rsqrt(jnp.mean(jnp.square(xf), axis=-1, keepdims=True) + RMS_EPS)
    return (y * g.astype(jnp.float32)).astype(x.dtype)


def modulate(x, shift, scale):
    return x * (1.0 + scale[:, None, :]) + shift[:, None, :]


def diff_attention(q, k, v, lam):
    b, s, h, _, dh = q.shape
    e = v.shape[-1]
    nb = s // Q_BLOCK
    qb = q.reshape(b, nb, Q_BLOCK, h, 2, dh).transpose(1, 0, 2, 3, 4, 5)
    key_pos = jnp.arange(s)
    scale = dh ** -0.5

    def one_block(args):
        q_blk, blk = args
        sc = jnp.einsum('bqhmd,bkhmd->bhmqk', q_blk, k,
                        preferred_element_type=jnp.float32) * scale
        q_pos = blk * Q_BLOCK + jnp.arange(Q_BLOCK)
        mask = key_pos[None, :] <= q_pos[:, None]
        sc = jnp.where(mask, sc, -jnp.inf)
        p = jax.nn.softmax(sc, axis=-1)
        w = p[:, :, 0] - lam * p[:, :, 1]
        return jnp.einsum('bhqk,bkhe->bqhe', w.astype(v.dtype), v)

    out = lax.map(one_block, (qb, jnp.arange(nb)))
    return out.transpose(1, 0, 2, 3, 4).reshape(b, s, h, e)


def spatial_gating(u, v, ln_g, ln_b, w_s, b_s):
    b, s, g, ch = v.shape
    v = layer_norm(v, ln_g.reshape(g, ch), ln_b.reshape(g, ch))
    vc = v.reshape(b, s // CHUNK, CHUNK, g, ch)
    causal = jnp.tril(jnp.ones((CHUNK, CHUNK), dtype=w_s.dtype))
    w = w_s * causal[None]
    gate = jnp.einsum('gts,bnsgc->bntgc', w, vc) + b_s.T[None, None, :, :, None]
    return u * gate.reshape(b, s, g, ch)


def setup_inputs(seed: int = 0) -> dict:
    key = jax.random.key(seed)
    ks = jax.random.split(key, 24)
    L, D = DEPTH, D_MODEL
    nrm = lambda k, shape, s: jax.random.normal(k, shape, jnp.float32) * s
    return {
        "x": nrm(ks[0], (BATCH, SEQ, D), 1.0),
        "c": nrm(ks[1], (BATCH, D), 1.0),
        "w_ada": nrm(ks[2], (L, D, 6 * D), 0.1 * D ** -0.5),
        "b_ada": nrm(ks[3], (L, 6 * D), 0.01),
        "w_in": nrm(ks[4], (L, D, D_IN), D ** -0.5),
        "lambda_q1": nrm(ks[5], (L, ATTN_HEAD_DIM), 0.1),
        "lambda_k1": nrm(ks[6], (L, ATTN_HEAD_DIM), 0.1),
        "lambda_q2": nrm(ks[7], (L, ATTN_HEAD_DIM), 0.1),
        "lambda_k2": nrm(ks[8], (L, ATTN_HEAD_DIM), 0.1),
        "subln_g": 1.0 + nrm(ks[9], (L, ATTN_V_DIM), 0.02),
        "sgu_ln_g": 1.0 + nrm(ks[10], (L, D_SGU), 0.02),
        "sgu_ln_b": nrm(ks[11], (L, D_SGU), 0.02),
        "w_spatial": nrm(ks[12], (L, SGU_GROUPS, CHUNK, CHUNK), 0.5 * CHUNK ** -0.5),
        "b_spatial": 1.0 + nrm(ks[13], (L, SGU_GROUPS, CHUNK), 0.1),
        "w_out": nrm(ks[14], (L, D_MIX, D), DEEPNORM_BETA * D_MIX ** -0.5),
        "ln1_g": 1.0 + nrm(ks[15], (L, D), 0.02),
        "ln1_b": nrm(ks[16], (L, D), 0.02),
        "w_gate": nrm(ks[17], (L, D, D_FF), D ** -0.5),
        "w_up": nrm(ks[18], (L, D, D_FF), D ** -0.5),
        "w_down": nrm(ks[19], (L, D_FF, D), DEEPNORM_BETA * D_FF ** -0.5),
        "ln2_g": 1.0 + nrm(ks[20], (L, D), 0.02),
        "ln2_b": nrm(ks[21], (L, D), 0.02),
    }


def reference(x, c, w_ada, b_ada, w_in, lambda_q1, lambda_k1, lambda_q2, lambda_k2,
              subln_g, sgu_ln_g, sgu_ln_b, w_spatial, b_spatial, w_out,
              ln1_g, ln1_b, w_gate, w_up, w_down, ln2_g, ln2_b):
    b, s, _ = x.shape
    c_act = jax.nn.silu(c)
    for l in range(DEPTH):
        mod = c_act @ w_ada[l] + b_ada[l]
        sh1, sc1, g1, sh2, sc2, g2 = jnp.split(mod, 6, axis=-1)

        h = modulate(x, sh1, sc1)
        proj = h @ w_in[l]
        q, k, v_a, z = jnp.split(proj, [D_ATTN, 2 * D_ATTN, 3 * D_ATTN], axis=-1)
        q = q.reshape(b, s, ATTN_HEADS, 2, ATTN_HEAD_DIM)
        k = k.reshape(b, s, ATTN_HEADS, 2, ATTN_HEAD_DIM)
        v_a = v_a.reshape(b, s, ATTN_HEADS, ATTN_V_DIM)

        lam_init = 0.8 - 0.6 * math.exp(-0.3 * l)
        lam = (jnp.exp(jnp.sum(lambda_q1[l].astype(jnp.float32) * lambda_k1[l].astype(jnp.float32)))
               - jnp.exp(jnp.sum(lambda_q2[l].astype(jnp.float32) * lambda_k2[l].astype(jnp.float32)))
               + lam_init)
        a = diff_attention(q, k, v_a, lam)
        a = rms_norm(a, subln_g[l]) * (1.0 - lam_init)

        z = jax.nn.gelu(z)
        u, vs = jnp.split(z, 2, axis=-1)
        u = u.reshape(b, s, SGU_GROUPS, SGU_GROUP_DIM)
        vs = vs.reshape(b, s, SGU_GROUPS, SGU_GROUP_DIM)
        g_out = spatial_gating(u, vs, sgu_ln_g[l], sgu_ln_b[l], w_spatial[l], b_spatial[l])

        mixed = jnp.concatenate([a.reshape(b, s, D_ATTN), g_out.reshape(b, s, D_SGU)], axis=-1)
        mixed = mixed @ w_out[l]
        x = layer_norm(DEEPNORM_ALPHA * x + (1.0 + g1)[:, None, :] * mixed, ln1_g[l], ln1_b[l])

        h = modulate(x, sh2, sc2)
        f = (jax.nn.silu(h @ w_gate[l]) * (h @ w_up[l])) @ w_down[l]
        x = layer_norm(DEEPNORM_ALPHA * x + (1.0 + g2)[:, None, :] * f, ln2_g[l], ln2_b[l])
    return x
```

```python
import functools
import math

import jax
import jax.numpy as jnp
from jax import lax
from jax.experimental import pallas as pl
from jax.experimental.pallas import tpu as pltpu

ATTN_HEADS = 4
ATTN_HEAD_DIM = 64
ATTN_V_DIM = 2 * ATTN_HEAD_DIM
SGU_GROUPS = 4
SGU_GROUP_DIM = 128
CHUNK = 128
LN_EPS = 1e-5
RMS_EPS = 1e-5

SUBLANES = 8
LANES = 128

PROJ_ROWS = 512
ATTN_BLOCK = 256
FFN_ROWS = 512
FFN_SPLITS = 2

_BF16 = jnp.bfloat16
_F32 = jnp.float32


def _dot(a, b):
    return jnp.dot(a, b, preferred_element_type=_F32)


def _dot_nt(a, b):
    return lax.dot_general(a, b, (((1,), (1,)), ((), ())), preferred_element_type=_F32)


def _layer_norm_rows(x, g, b):
    mu = jnp.mean(x, axis=-1, keepdims=True)
    xc = x - mu
    var = jnp.mean(xc * xc, axis=-1, keepdims=True)
    return xc * lax.rsqrt(var + LN_EPS) * g + b


def _mod_kernel(c_ref, w_ref, b_ref, lam_vec_ref, mod_ref, lam_ref, *, lam_init):
    c = c_ref[...]
    c_act = c * jax.nn.sigmoid(c)
    mod_ref[...] = jnp.dot(c_act, w_ref[...], preferred_element_type=_F32,
                           precision=lax.Precision.HIGHEST) + b_ref[...]
    lv = lam_vec_ref[...]
    s1 = jnp.sum(lv[0:1] * lv[1:2], axis=-1, keepdims=True)
    s2 = jnp.sum(lv[2:3] * lv[3:4], axis=-1, keepdims=True)
    lam = jnp.exp(s1) - jnp.exp(s2) + lam_init
    lam_ref[...] = jnp.broadcast_to(lam, lam_ref.shape)


def _adaln_mod(c, w_ada, b_ada, lam_vecs, lam_init):
    bsz, d = c.shape
    n = w_ada.shape[1]
    c_pad = jnp.zeros((SUBLANES, d), _F32).at[:bsz].set(c)
    mod, lam = pl.pallas_call(
        functools.partial(_mod_kernel, lam_init=lam_init),
        grid=(n // d,),
        in_specs=[
            pl.BlockSpec((SUBLANES, d), lambda j: (0, 0)),
            pl.BlockSpec((d, d), lambda j: (0, j)),
            pl.BlockSpec((1, d), lambda j: (0, j)),
            pl.BlockSpec(lam_vecs.shape, lambda j: (0, 0)),
        ],
        out_specs=[
            pl.BlockSpec((SUBLANES, d), lambda j: (0, j)),
            pl.BlockSpec((SUBLANES, LANES), lambda j: (0, 0)),
        ],
        out_shape=[
            jax.ShapeDtypeStruct((SUBLANES, n), _F32),
            jax.ShapeDtypeStruct((SUBLANES, LANES), _F32),
        ],
        name="adaln_mod",
    )(c_pad, w_ada, b_ada.reshape(1, n), lam_vecs)
    return mod[:bsz].reshape(bsz, n // d, d), lam[0, :1]


def _proj_kernel(x_ref, mod_ref, wqt_ref, wk_ref, wvt_ref, wz_ref, lng_ref, lnb_ref,
                 wsp_ref, bsp_ref, qt_ref, k_ref, vt_ref, g_ref, *, q_scale):
    rows = x_ref.shape[0]
    mod = mod_ref[...]
    h = (x_ref[...] * (1.0 + mod[1:2]) + mod[0:1]).astype(_BF16)

    k_ref[...] = _dot(h, wk_ref[...]).astype(_BF16)
    qt_ref[...] = (_dot_nt(wqt_ref[...], h) * q_scale).astype(_BF16)
    vt = _dot_nt(wvt_ref[...], h).astype(_BF16)
    blk = vt_ref.shape[-1]
    for j in range(rows // blk):
        vt_ref[j] = vt[:, j * blk:(j + 1) * blk]

    z = jax.nn.gelu(_dot(h, wz_ref[...]), approximate=True)
    d_sgu = SGU_GROUPS * SGU_GROUP_DIM
    n_chunks = rows // CHUNK
    t_idx = lax.broadcasted_iota(jnp.int32, (CHUNK, CHUNK), 0)
    s_idx = lax.broadcasted_iota(jnp.int32, (CHUNK, CHUNK), 1)
    causal = s_idx <= t_idx
    for g in range(SGU_GROUPS):
        lo = g * SGU_GROUP_DIM
        hi = lo + SGU_GROUP_DIM
        u = z[:, lo:hi]
        vs = z[:, d_sgu + lo:d_sgu + hi]
        vn = _layer_norm_rows(vs, lng_ref[:, lo:hi], lnb_ref[:, lo:hi]).astype(_BF16)
        v_cat = jnp.concatenate(
            [vn[c * CHUNK:(c + 1) * CHUNK] for c in range(n_chunks)], axis=1)
        w = jnp.where(causal, wsp_ref[g], 0.0).astype(_BF16)
        gate = _dot(w, v_cat) + bsp_ref[:, g:g + 1]
        for c in range(n_chunks):
            g_ref[c * CHUNK:(c + 1) * CHUNK, lo:hi] = (
                u[c * CHUNK:(c + 1) * CHUNK] * gate[:, c * CHUNK:(c + 1) * CHUNK]
            ).astype(_BF16)


def _in_projection(x, mod, wqt, wk, wvt, wz, ln_g, ln_b, w_sp, b_sp_t):
    bsz, seq, d = x.shape
    d_attn = wk.shape[1]
    d_z = wz.shape[1]
    rows = PROJ_ROWS
    blk = ATTN_BLOCK
    const = lambda *shape: pl.BlockSpec(shape, lambda b, i: (0,) * len(shape),
                                        pipeline_mode=pl.Buffered(1))
    return pl.pallas_call(
        functools.partial(_proj_kernel, q_scale=ATTN_HEAD_DIM ** -0.5),
        grid=(bsz, seq // rows),
        in_specs=[
            pl.BlockSpec((None, rows, d), lambda b, i: (b, i, 0)),
            pl.BlockSpec((None,) + mod.shape[1:], lambda b, i: (b, 0, 0)),
            const(d_attn, d), const(d, d_attn), const(d_attn, d), const(d, d_z),
            const(1, d_z // 2), const(1, d_z // 2),
            const(SGU_GROUPS, CHUNK, CHUNK), const(CHUNK, SGU_GROUPS),
        ],
        out_specs=[
            pl.BlockSpec((None, d_attn, rows), lambda b, i: (b, 0, i)),
            pl.BlockSpec((None, rows, d_attn), lambda b, i: (b, i, 0)),
            pl.BlockSpec((None, rows // blk, d_attn, blk), lambda b, i: (b, i, 0, 0)),
            pl.BlockSpec((None, rows, d_z // 2), lambda b, i: (b, i, 0)),
        ],
        out_shape=[
            jax.ShapeDtypeStruct((bsz, d_attn, seq), _BF16),
            jax.ShapeDtypeStruct((bsz, seq, d_attn), _BF16),
            jax.ShapeDtypeStruct((bsz, seq // blk, d_attn, blk), _BF16),
            jax.ShapeDtypeStruct((bsz, seq, d_z // 2), _BF16),
        ],
        compiler_params=pltpu.CompilerParams(
            dimension_semantics=("parallel", "parallel"),
            vmem_limit_bytes=48 * 1024 * 1024),
        name="in_projection",
    )(x, mod, wqt, wk, wvt, wz, ln_g, ln_b, w_sp, b_sp_t)


def _attn_kernel(lam_ref, qt_ref, k_ref, vt_ref, g_ref, o_ref, m_ref, l_ref, acc_ref,
                 *, out_scale):
    blk = qt_ref.shape[-1]
    i = pl.program_id(2)

    qt = qt_ref[...]
    row = lax.broadcasted_iota(jnp.int32, qt.shape, 0)
    zero = jnp.zeros_like(qt)
    qz = jnp.concatenate([jnp.where(row < ATTN_HEAD_DIM, qt, zero),
                          jnp.where(row >= ATTN_HEAD_DIM, qt, zero)], axis=1)

    m_ref[...] = jnp.full(m_ref.shape, -jnp.inf, _F32)
    l_ref[...] = jnp.zeros(l_ref.shape, _F32)
    acc_ref[...] = jnp.zeros(acc_ref.shape, _F32)

    def step(kb, masked):
        start = pl.multiple_of(kb * blk, blk)
        s = _dot(k_ref[pl.ds(start, blk), :], qz)
        if masked:
            key = lax.broadcasted_iota(jnp.int32, s.shape, 0)
            qry = lax.broadcasted_iota(jnp.int32, s.shape, 1) % blk
            s = jnp.where(key <= qry, s, -jnp.inf)
        m_prev = m_ref[...]
        m_new = jnp.maximum(m_prev, jnp.max(s, axis=0, keepdims=True))
        alpha = jnp.exp(m_prev - m_new)
        e = jnp.exp(s - m_new)
        l_ref[...] = alpha * l_ref[...] + jnp.sum(e, axis=0, keepdims=True)
        acc_ref[...] = alpha * acc_ref[...] + _dot(vt_ref[kb], e.astype(_BF16))
        m_ref[...] = m_new

    def body(kb, carry):
        step(kb, False)
        return carry

    lax.fori_loop(0, i, body, 0)
    step(i, True)

    lam = lam_ref[0]
    o = acc_ref[...] / l_ref[...]
    o = o[:, :blk] - lam * o[:, blk:]
    ms = jnp.mean(o * o, axis=0, keepdims=True)
    y = o * lax.rsqrt(ms + RMS_EPS) * g_ref[...] * out_scale
    o_ref[...] = y.T.astype(o_ref.dtype)


def _diff_attention(lam, qt, k, vt, subln_g_col, out_scale):
    bsz, d_attn, seq = qt.shape
    blk = ATTN_BLOCK
    e = ATTN_V_DIM
    return pl.pallas_call(
        functools.partial(_attn_kernel, out_scale=out_scale),
        grid=(bsz, ATTN_HEADS, seq // blk),
        in_specs=[
            pl.BlockSpec(memory_space=pltpu.SMEM),
            pl.BlockSpec((None, e, blk), lambda b, h, i: (b, h, i)),
            pl.BlockSpec((None, seq, e), lambda b, h, i: (b, 0, h)),
            pl.BlockSpec((None, seq // blk, e, blk), lambda b, h, i: (b, 0, h, 0)),
            pl.BlockSpec((e, 1), lambda b, h, i: (0, 0)),
        ],
        out_specs=pl.BlockSpec((None, blk, e), lambda b, h, i: (b, i, h)),
        out_shape=jax.ShapeDtypeStruct((bsz, seq, d_attn), _BF16),
        scratch_shapes=[
            pltpu.VMEM((1, 2 * blk), _F32),
            pltpu.VMEM((1, 2 * blk), _F32),
            pltpu.VMEM((e, 2 * blk), _F32),
        ],
        compiler_params=pltpu.CompilerParams(
            dimension_semantics=("parallel", "parallel", "arbitrary"),
            vmem_limit_bytes=32 * 1024 * 1024),
        name="diff_attention",
    )(lam, qt, k, vt, subln_g_col)


def _ffn_kernel(x_ref, a_ref, g_ref, mod_ref, wo_a_ref, wo_g_ref, ln1g_ref, ln1b_ref,
                wgate_ref, wup_ref, wdown_ref, ln2g_ref, ln2b_ref, o_ref, *, alpha):
    mod = mod_ref[...]
    mixed = _dot(a_ref[...], wo_a_ref[...]) + _dot(g_ref[...], wo_g_ref[...])
    x1 = _layer_norm_rows(alpha * x_ref[...] + (1.0 + mod[2:3]) * mixed,
                          ln1g_ref[...], ln1b_ref[...])
    h = (x1 * (1.0 + mod[4:5]) + mod[3:4]).astype(_BF16)
    d_ff = wgate_ref.shape[1]
    width = d_ff // FFN_SPLITS
    f = None
    for j in range(FFN_SPLITS):
        cols = slice(j * width, (j + 1) * width)
        gate = _dot(h, wgate_ref[:, cols])
        act = (gate * jax.nn.sigmoid(gate) * _dot(h, wup_ref[:, cols])).astype(_BF16)
        part = _dot(act, wdown_ref[cols, :])
        f = part if f is None else f + part
    o_ref[...] = _layer_norm_rows(alpha * x1 + (1.0 + mod[5:6]) * f,
                                  ln2g_ref[...], ln2b_ref[...])


def _out_ffn(x, a, g_out, mod, wo_a, wo_g, ln1_g, ln1_b, w_gate, w_up, w_down, ln2_g, ln2_b,
             alpha):
    bsz, seq, d = x.shape
    rows = FFN_ROWS
    d_half = a.shape[-1]
    d_ff = w_gate.shape[1]
    const = lambda *shape: pl.BlockSpec(shape, lambda b, i: (0,) * len(shape),
                                        pipeline_mode=pl.Buffered(1))
    return pl.pallas_call(
        functools.partial(_ffn_kernel, alpha=alpha),
        grid=(bsz, seq // rows),
        in_specs=[
            pl.BlockSpec((None, rows, d), lambda b, i: (b, i, 0)),
            pl.BlockSpec((None, rows, d_half), lambda b, i: (b, i, 0)),
            pl.BlockSpec((None, rows, d_half), lambda b, i: (b, i, 0)),
            pl.BlockSpec((None,) + mod.shape[1:], lambda b, i: (b, 0, 0)),
            const(d_half, d), const(d_half, d), const(1, d), const(1, d),
            const(d, d_ff), const(d, d_ff), const(d_ff, d), const(1, d), const(1, d),
        ],
        out_specs=pl.BlockSpec((None, rows, d), lambda b, i: (b, i, 0)),
        out_shape=jax.ShapeDtypeStruct((bsz, seq, d), x.dtype),
        compiler_params=pltpu.CompilerParams(
            dimension_semantics=("parallel", "parallel"),
            vmem_limit_bytes=56 * 1024 * 1024),
        name="out_ffn",
    )(x, a, g_out, mod, wo_a, wo_g, ln1_g, ln1_b, w_gate, w_up, w_down, ln2_g, ln2_b)


def kernel(x, c, w_ada, b_ada, w_in, lambda_q1, lambda_k1, lambda_q2, lambda_k2, subln_g,
           sgu_ln_g, sgu_ln_b, w_spatial, b_spatial, w_out, ln1_g, ln1_b, w_gate, w_up, w_down,
           ln2_g, ln2_b):
    depth = w_in.shape[0]
    d = x.shape[-1]
    d_attn = ATTN_HEADS * ATTN_V_DIM
    alpha = (2 * depth) ** 0.25
    for l in range(depth):
        lam_init = 0.8 - 0.6 * math.exp(-0.3 * l)
        lam_vecs = jnp.stack([lambda_q1[l], lambda_k1[l], lambda_q2[l], lambda_k2[l]])
        mod, lam = _adaln_mod(c, w_ada[l], b_ada[l], lam_vecs.astype(_F32), lam_init)

        w = w_in[l].astype(_BF16)
        wqt = w[:, :d_attn].T
        wk = w[:, d_attn:2 * d_attn]
        wvt = w[:, 2 * d_attn:3 * d_attn].T
        wz = w[:, 3 * d_attn:]
        qt, k, vt, g_out = _in_projection(
            x, mod, wqt, wk, wvt, wz,
            sgu_ln_g[l].reshape(1, -1), sgu_ln_b[l].reshape(1, -1),
            w_spatial[l], b_spatial[l].T)

        a = _diff_attention(lam, qt, k, vt, subln_g[l].reshape(-1, 1), 1.0 - lam_init)

        wo = w_out[l].astype(_BF16)
        x = _out_ffn(
            x, a, g_out, mod, wo[:d_attn], wo[d_attn:],
            ln1_g[l].reshape(1, d), ln1_b[l].reshape(1, d),
            w_gate[l].astype(_BF16), w_up[l].astype(_BF16), w_down[l].astype(_BF16),
            ln2_g[l].reshape(1, d), ln2_b[l].reshape(1, d), alpha)
    return x
```

```python
import functools
import math

import jax
import jax.numpy as jnp
from jax import lax
from jax.experimental import pallas as pl
from jax.experimental.pallas import tpu as pltpu

ATTN_HEADS = 4
ATTN_HEAD_DIM = 64
ATTN_V_DIM = 2 * ATTN_HEAD_DIM
SGU_GROUPS = 4
SGU_GROUP_DIM = 128
CHUNK = 128
LN_EPS = 1e-5
RMS_EPS = 1e-5

SUBLANES = 8
LANES = 128

PROJ_ROWS = 512
ATTN_BLOCK = 256
FFN_ROWS = 512
FFN_SPLITS = 2

_BF16 = jnp.bfloat16
_F32 = jnp.float32


def _dot(a, b):
    return jnp.dot(a, b, preferred_element_type=_F32)


def _dot_nt(a, b):
    return lax.dot_general(a, b, (((1,), (1,)), ((), ())), preferred_element_type=_F32)


def _layer_norm_rows(x, g, b):
    mu = jnp.mean(x, axis=-1, keepdims=True)
    xc = x - mu
    var = jnp.mean(xc * xc, axis=-1, keepdims=True)
    return xc * lax.rsqrt(var + LN_EPS) * g + b


def _mod_kernel(c_ref, w_ref, b_ref, lam_vec_ref, mod_ref, lam_ref, *, lam_init):
    c = c_ref[...]
    c_act = c * jax.nn.sigmoid(c)
    mod_ref[...] = jnp.dot(c_act, w_ref[...], preferred_element_type=_F32,
                           precision=lax.Precision.HIGHEST) + b_ref[...]
    lv = lam_vec_ref[...]
    s1 = jnp.sum(lv[0:1] * lv[1:2], axis=-1, keepdims=True)
    s2 = jnp.sum(lv[2:3] * lv[3:4], axis=-1, keepdims=True)
    lam = jnp.exp(s1) - jnp.exp(s2) + lam_init
    lam_ref[...] = jnp.broadcast_to(lam, lam_ref.shape)


def _adaln_mod(c, w_ada, b_ada, lam_vecs, lam_init):
    bsz, d = c.shape
    n = w_ada.shape[1]
    c_pad = jnp.zeros((SUBLANES, d), _F32).at[:bsz].set(c)
    mod, lam = pl.pallas_call(
        functools.partial(_mod_kernel, lam_init=lam_init),
        grid=(n // d,),
        in_specs=[
            pl.BlockSpec((SUBLANES, d), lambda j: (0, 0)),
            pl.BlockSpec((d, d), lambda j: (0, j)),
            pl.BlockSpec((1, d), lambda j: (0, j)),
            pl.BlockSpec(lam_vecs.shape, lambda j: (0, 0)),
        ],
        out_specs=[
            pl.BlockSpec((SUBLANES, d), lambda j: (0, j)),
            pl.BlockSpec((SUBLANES, LANES), lambda j: (0, 0)),
        ],
        out_shape=[
            jax.ShapeDtypeStruct((SUBLANES, n), _F32),
            jax.ShapeDtypeStruct((SUBLANES, LANES), _F32),
        ],
        name="adaln_mod",
    )(c_pad, w_ada, b_ada.reshape(1, n), lam_vecs)
    return mod[:bsz].reshape(bsz, n // d, d), lam[0, :1]


def _proj_kernel(x_ref, mod_ref, wqt_ref, wk_ref, wvt_ref, wz_ref, lng_ref, lnb_ref,
                 wsp_ref, bsp_ref, qt_ref, k_ref, vt_ref, g_ref, *, q_scale):
    rows = x_ref.shape[0]
    mod = mod_ref[...]
    h = (x_ref[...] * (1.0 + mod[1:2]) + mod[0:1]).astype(_BF16)

    k_ref[...] = _dot(h, wk_ref[...]).astype(_BF16)
    qt_ref[...] = (_dot_nt(wqt_ref[...], h) * q_scale).astype(_BF16)
    vt = _dot_nt(wvt_ref[...], h).astype(_BF16)
    blk = vt_ref.shape[-1]
    for j in range(rows // blk):
        vt_ref[j] = vt[:, j * blk:(j + 1) * blk]

    z = jax.nn.gelu(_dot(h, wz_ref[...]), approximate=True)
    d_sgu = SGU_GROUPS * SGU_GROUP_DIM
    n_chunks = rows // CHUNK
    t_idx = lax.broadcasted_iota(jnp.int32, (CHUNK, CHUNK), 0)
    s_idx = lax.broadcasted_iota(jnp.int32, (CHUNK, CHUNK), 1)
    causal = s_idx <= t_idx
    for g in range(SGU_GROUPS):
        lo = g * SGU_GROUP_DIM
        hi = lo + SGU_GROUP_DIM
        u = z[:, lo:hi]
        vs = z[:, d_sgu + lo:d_sgu + hi]
        vn = _layer_norm_rows(vs, lng_ref[:, lo:hi], lnb_ref[:, lo:hi]).astype(_BF16)
        v_cat = jnp.concatenate(
            [vn[c * CHUNK:(c + 1) * CHUNK] for c in range(n_chunks)], axis=1)
        w = jnp.where(causal, wsp_ref[g], 0.0).astype(_BF16)
        gate = _dot(w, v_cat) + bsp_ref[:, g:g + 1]
        for c in range(n_chunks):
            g_ref[c * CHUNK:(c + 1) * CHUNK, lo:hi] = (
                u[c * CHUNK:(c + 1) * CHUNK] * gate[:, c * CHUNK:(c + 1) * CHUNK]
            ).astype(_BF16)


def _in_projection(x, mod, wqt, wk, wvt, wz, ln_g, ln_b, w_sp, b_sp_t):
    bsz, seq, d = x.shape
    d_attn = wk.shape[1]
    d_z = wz.shape[1]
    rows = PROJ_ROWS
    blk = ATTN_BLOCK
    const = lambda *shape: pl.BlockSpec(shape, lambda b, i: (0,) * len(shape),
                                        pipeline_mode=pl.Buffered(1))
    return pl.pallas_call(
        functools.partial(_proj_kernel, q_scale=ATTN_HEAD_DIM ** -0.5),
        grid=(bsz, seq // rows),
        in_specs=[
            pl.BlockSpec((None, rows, d), lambda b, i: (b, i, 0)),
            pl.BlockSpec((None,) + mod.shape[1:], lambda b, i: (b, 0, 0)),
            const(d_attn, d), const(d, d_attn), const(d_attn, d), const(d, d_z),
            const(1, d_z // 2), const(1, d_z // 2),
            const(SGU_GROUPS, CHUNK, CHUNK), const(CHUNK, SGU_GROUPS),
        ],
        out_specs=[
            pl.BlockSpec((None, d_attn, rows), lambda b, i: (b, 0, i)),
            pl.BlockSpec((None, rows, d_attn), lambda b, i: (b, i, 0)),
            pl.BlockSpec((None, rows // blk, d_attn, blk), lambda b, i: (b, i, 0, 0)),
            pl.BlockSpec((None, rows, d_z // 2), lambda b, i: (b, i, 0)),
        ],
        out_shape=[
            jax.ShapeDtypeStruct((bsz, d_attn, seq), _BF16),
            jax.ShapeDtypeStruct((bsz, seq, d_attn), _BF16),
            jax.ShapeDtypeStruct((bsz, seq // blk, d_attn, blk), _BF16),
            jax.ShapeDtypeStruct((bsz, seq, d_z // 2), _BF16),
        ],
        compiler_params=pltpu.CompilerParams(
            dimension_semantics=("parallel", "parallel"),
            vmem_limit_bytes=48 * 1024 * 1024),
        name="in_projection",
    )(x, mod, wqt, wk, wvt, wz, ln_g, ln_b, w_sp, b_sp_t)


def _attn_kernel(lam_ref, qt_ref, k_ref, vt_ref, g_ref, o_ref,
                 qz_ref, s_ref, cmax_ref, m_ref, l_ref, acc_ref, *, out_scale):
    blk = qt_ref.shape[-1]
    e_dim = ATTN_V_DIM
    i = pl.program_id(1)
    heads = range(ATTN_HEADS)
    cols = [slice(h * e_dim, (h + 1) * e_dim) for h in heads]

    for h in heads:
        qt = qt_ref[cols[h], :]
        row = lax.broadcasted_iota(jnp.int32, qt.shape, 0)
        zero = jnp.zeros_like(qt)
        qz_ref[h, :, :blk] = jnp.where(row < ATTN_HEAD_DIM, qt, zero)
        qz_ref[h, :, blk:] = jnp.where(row >= ATTN_HEAD_DIM, qt, zero)

    m_ref[...] = jnp.full(m_ref.shape, -jnp.inf, _F32)
    l_ref[...] = jnp.zeros(l_ref.shape, _F32)
    acc_ref[...] = jnp.zeros(acc_ref.shape, _F32)

    def scores(h, kb, masked):
        start = pl.multiple_of(kb * blk, blk)
        s = _dot(k_ref[pl.ds(start, blk), cols[h]], qz_ref[h])
        if masked:
            key = lax.broadcasted_iota(jnp.int32, s.shape, 0)
            qry = lax.broadcasted_iota(jnp.int32, s.shape, 1) % blk
            s = jnp.where(key <= qry, s, -jnp.inf)
        return s

    def stash(h, s):
        s_ref[h] = s
        cmax_ref[h] = jnp.max(s, axis=0, keepdims=True)

    def consume(h, kb):
        m_prev = m_ref[h]
        m_new = jnp.maximum(m_prev, cmax_ref[h])
        alpha = jnp.exp(m_prev - m_new)
        e = jnp.exp(s_ref[h] - m_new)
        l_ref[h] = alpha * l_ref[h] + jnp.sum(e, axis=0, keepdims=True)
        m_ref[h] = m_new
        pv = _dot(vt_ref[kb, cols[h], :], e.astype(_BF16))
        acc_ref[h] = alpha * acc_ref[h] + pv

    for h in heads:
        stash(h, scores(h, i, True))

    def body(t, carry):
        cur = jnp.where(t == 0, i, t - 1)
        nxt = {0: scores(0, t, False), 1: scores(1, t, False)}
        for h in heads:
            consume(h, cur)
            stash(h, nxt.pop(h))
            if h + 2 < ATTN_HEADS:
                nxt[h + 2] = scores(h + 2, t, False)
        return carry

    lax.fori_loop(0, i, body, 0)
    last = jnp.where(i == 0, i, i - 1)
    for h in heads:
        consume(h, last)

    lam = lam_ref[0]
    for h in heads:
        o = acc_ref[h] / l_ref[h]
        o = o[:, :blk] - lam * o[:, blk:]
        ms = jnp.mean(o * o, axis=0, keepdims=True)
        y = o * lax.rsqrt(ms + RMS_EPS) * g_ref[...] * out_scale
        o_ref[:, cols[h]] = y.T.astype(o_ref.dtype)


def _diff_attention(lam, qt, k, vt, subln_g_col, out_scale):
    bsz, d_attn, seq = qt.shape
    blk = ATTN_BLOCK
    e = ATTN_V_DIM
    return pl.pallas_call(
        functools.partial(_attn_kernel, out_scale=out_scale),
        grid=(bsz, seq // blk),
        in_specs=[
            pl.BlockSpec(memory_space=pltpu.SMEM),
            pl.BlockSpec((None, d_attn, blk), lambda b, i: (b, 0, i)),
            pl.BlockSpec((None, seq, d_attn), lambda b, i: (b, 0, 0)),
            pl.BlockSpec((None, seq // blk, d_attn, blk), lambda b, i: (b, 0, 0, 0)),
            pl.BlockSpec((e, 1), lambda b, i: (0, 0)),
        ],
        out_specs=pl.BlockSpec((None, blk, d_attn), lambda b, i: (b, i, 0)),
        out_shape=jax.ShapeDtypeStruct((bsz, seq, d_attn), _BF16),
        scratch_shapes=[
            pltpu.VMEM((ATTN_HEADS, e, 2 * blk), _BF16),
            pltpu.VMEM((ATTN_HEADS, blk, 2 * blk), _F32),
            pltpu.VMEM((ATTN_HEADS, 1, 2 * blk), _F32),
            pltpu.VMEM((ATTN_HEADS, 1, 2 * blk), _F32),
            pltpu.VMEM((ATTN_HEADS, 1, 2 * blk), _F32),
            pltpu.VMEM((ATTN_HEADS, e, 2 * blk), _F32),
        ],
        compiler_params=pltpu.CompilerParams(
            dimension_semantics=("parallel", "arbitrary"),
            vmem_limit_bytes=48 * 1024 * 1024),
        name="diff_attention",
    )(lam, qt, k, vt, subln_g_col)


def _ffn_kernel(x_ref, a_ref, g_ref, mod_ref, wo_a_ref, wo_g_ref, ln1g_ref, ln1b_ref,
                wgate_ref, wup_ref, wdown_ref, ln2g_ref, ln2b_ref, o_ref, *, alpha):
    mod = mod_ref[...]
    mixed = _dot(a_ref[...], wo_a_ref[...]) + _dot(g_ref[...], wo_g_ref[...])
    x1 = _layer_norm_rows(alpha * x_ref[...] + (1.0 + mod[2:3]) * mixed,
                          ln1g_ref[...], ln1b_ref[...])
    h = (x1 * (1.0 + mod[4:5]) + mod[3:4]).astype(_BF16)
    d_ff = wgate_ref.shape[1]
    width = d_ff // FFN_SPLITS
    f = None
    for j in range(FFN_SPLITS):
        cols = slice(j * width, (j + 1) * width)
        gate = _dot(h, wgate_ref[:, cols])
        act = (gate * jax.nn.sigmoid(gate) * _dot(h, wup_ref[:, cols])).astype(_BF16)
        part = _dot(act, wdown_ref[cols, :])
        f = part if f is None else f + part
    o_ref[...] = _layer_norm_rows(alpha * x1 + (1.0 + mod[5:6]) * f,
                                  ln2g_ref[...], ln2b_ref[...])


def _out_ffn(x, a, g_out, mod, wo_a, wo_g, ln1_g, ln1_b, w_gate, w_up, w_down, ln2_g, ln2_b,
             alpha):
    bsz, seq, d = x.shape
    rows = FFN_ROWS
    d_half = a.shape[-1]
    d_ff = w_gate.shape[1]
    const = lambda *shape: pl.BlockSpec(shape, lambda b, i: (0,) * len(shape),
                                        pipeline_mode=pl.Buffered(1))
    return pl.pallas_call(
        functools.partial(_ffn_kernel, alpha=alpha),
        grid=(bsz, seq // rows),
        in_specs=[
            pl.BlockSpec((None, rows, d), lambda b, i: (b, i, 0)),
            pl.BlockSpec((None, rows, d_half), lambda b, i: (b, i, 0)),
            pl.BlockSpec((None, rows, d_half), lambda b, i: (b, i, 0)),
            pl.BlockSpec((None,) + mod.shape[1:], lambda b, i: (b, 0, 0)),
            const(d_half, d), const(d_half, d), const(1, d), const(1, d),
            const(d, d_ff), const(d, d_ff), const(d_ff, d), const(1, d), const(1, d),
        ],
        out_specs=pl.BlockSpec((None, rows, d), lambda b, i: (b, i, 0)),
        out_shape=jax.ShapeDtypeStruct((bsz, seq, d), x.dtype),
        compiler_params=pltpu.CompilerParams(
            dimension_semantics=("parallel", "parallel"),
            vmem_limit_bytes=56 * 1024 * 1024),
        name="out_ffn",
    )(x, a, g_out, mod, wo_a, wo_g, ln1_g, ln1_b, w_gate, w_up, w_down, ln2_g, ln2_b)


def kernel(x, c, w_ada, b_ada, w_in, lambda_q1, lambda_k1, lambda_q2, lambda_k2, subln_g,
           sgu_ln_g, sgu_ln_b, w_spatial, b_spatial, w_out, ln1_g, ln1_b, w_gate, w_up, w_down,
           ln2_g, ln2_b):
    depth = w_in.shape[0]
    d = x.shape[-1]
    d_attn = ATTN_HEADS * ATTN_V_DIM
    alpha = (2 * depth) ** 0.25
    for l in range(depth):
        lam_init = 0.8 - 0.6 * math.exp(-0.3 * l)
        lam_vecs = jnp.stack([lambda_q1[l], lambda_k1[l], lambda_q2[l], lambda_k2[l]])
        mod, lam = _adaln_mod(c, w_ada[l], b_ada[l], lam_vecs.astype(_F32), lam_init)

        w = w_in[l].astype(_BF16)
        wqt = w[:, :d_attn].T
        wk = w[:, d_attn:2 * d_attn]
        wvt = w[:, 2 * d_attn:3 * d_attn].T
        wz = w[:, 3 * d_attn:]
        qt, k, vt, g_out = _in_projection(
            x, mod, wqt, wk, wvt, wz,
            sgu_ln_g[l].reshape(1, -1), sgu_ln_b[l].reshape(1, -1),
            w_spatial[l], b_spatial[l].T)

        a = _diff_attention(lam, qt, k, vt, subln_g[l].reshape(-1, 1), 1.0 - lam_init)

        wo = w_out[l].astype(_BF16)
        x = _out_ffn(
            x, a, g_out, mod, wo[:d_attn], wo[d_attn:],
            ln1_g[l].reshape(1, d), ln1_b[l].reshape(1, d),
            w_gate[l].astype(_BF16), w_up[l].astype(_BF16), w_down[l].astype(_BF16),
            ln2_g[l].reshape(1, d), ln2_b[l].reshape(1, d), alpha)
    return x
```

```python
import functools
import math

import jax
import jax.numpy as jnp
from jax import lax
from jax.experimental import pallas as pl
from jax.experimental.pallas import tpu as pltpu

ATTN_HEADS = 4
ATTN_HEAD_DIM = 64
ATTN_V_DIM = 2 * ATTN_HEAD_DIM
SGU_GROUPS = 4
SGU_GROUP_DIM = 128
CHUNK = 128
LN_EPS = 1e-5
RMS_EPS = 1e-5

SUBLANES = 8
BF16_SUBLANES = 16
LANES = 128

PROJ_ROWS = 512
ATTN_BLOCK = 512
FFN_ROWS = 512
FFN_SPLITS = 2

_BF16 = jnp.bfloat16
_F32 = jnp.float32


def _dot(a, b):
    return jnp.dot(a, b, preferred_element_type=_F32)


def _dot_nt(a, b):
    return lax.dot_general(a, b, (((1,), (1,)), ((), ())), preferred_element_type=_F32)


def _layer_norm_rows(x, g, b):
    mu = jnp.mean(x, axis=-1, keepdims=True)
    xc = x - mu
    var = jnp.mean(xc * xc, axis=-1, keepdims=True)
    return xc * lax.rsqrt(var + LN_EPS) * g + b


def _mod_kernel(c_ref, w_ref, b_ref, lam_vec_ref, mod_ref, lam_ref, *, lam_init):
    c = c_ref[...]
    c_act = c * jax.nn.sigmoid(c)
    mod_ref[...] = jnp.dot(c_act, w_ref[...], preferred_element_type=_F32,
                           precision=lax.Precision.HIGHEST) + b_ref[...]
    lv = lam_vec_ref[...]
    s1 = jnp.sum(lv[0:1] * lv[1:2], axis=-1, keepdims=True)
    s2 = jnp.sum(lv[2:3] * lv[3:4], axis=-1, keepdims=True)
    lam = jnp.exp(s1) - jnp.exp(s2) + lam_init
    lam_ref[...] = jnp.broadcast_to(lam, lam_ref.shape)


def _adaln_mod(c, w_ada, b_ada, lam_vecs, lam_init):
    bsz, d = c.shape
    n = w_ada.shape[1]
    c_pad = jnp.zeros((SUBLANES, d), _F32).at[:bsz].set(c)
    mod, lam = pl.pallas_call(
        functools.partial(_mod_kernel, lam_init=lam_init),
        grid=(n // d,),
        in_specs=[
            pl.BlockSpec((SUBLANES, d), lambda j: (0, 0)),
            pl.BlockSpec((d, d), lambda j: (0, j)),
            pl.BlockSpec((1, d), lambda j: (0, j)),
            pl.BlockSpec(lam_vecs.shape, lambda j: (0, 0)),
        ],
        out_specs=[
            pl.BlockSpec((SUBLANES, d), lambda j: (0, j)),
            pl.BlockSpec((SUBLANES, LANES), lambda j: (0, 0)),
        ],
        out_shape=[
            jax.ShapeDtypeStruct((SUBLANES, n), _F32),
            jax.ShapeDtypeStruct((SUBLANES, LANES), _F32),
        ],
        name="adaln_mod",
    )(c_pad, w_ada, b_ada.reshape(1, n), lam_vecs)
    return mod[:bsz].reshape(bsz, n // d, d), lam[0, :1]


def _proj_kernel(x_ref, mod_ref, wqt_ref, wk_ref, wvt_ref, wz_ref, lng_ref, lnb_ref,
                 wsp_ref, bsp_ref, qt_ref, k_ref, vt_ref, g_ref, *, q_scale):
    rows = x_ref.shape[0]
    mod = mod_ref[...]
    h = (x_ref[...] * (1.0 + mod[1:2]) + mod[0:1]).astype(_BF16)

    k_ref[...] = _dot(h, wk_ref[...]).astype(_BF16)
    qt_ref[...] = (_dot_nt(wqt_ref[...], h) * q_scale).astype(_BF16)
    vt = _dot_nt(wvt_ref[...], h).astype(_BF16)
    blk = vt_ref.shape[-1]
    for j in range(rows // blk):
        vt_ref[j] = vt[:, j * blk:(j + 1) * blk]

    z = jax.nn.gelu(_dot(h, wz_ref[...]), approximate=True)
    d_sgu = SGU_GROUPS * SGU_GROUP_DIM
    n_chunks = rows // CHUNK
    t_idx = lax.broadcasted_iota(jnp.int32, (CHUNK, CHUNK), 0)
    s_idx = lax.broadcasted_iota(jnp.int32, (CHUNK, CHUNK), 1)
    causal = s_idx <= t_idx
    for g in range(SGU_GROUPS):
        lo = g * SGU_GROUP_DIM
        hi = lo + SGU_GROUP_DIM
        u = z[:, lo:hi]
        vs = z[:, d_sgu + lo:d_sgu + hi]
        vn = _layer_norm_rows(vs, lng_ref[:, lo:hi], lnb_ref[:, lo:hi]).astype(_BF16)
        v_cat = jnp.concatenate(
            [vn[c * CHUNK:(c + 1) * CHUNK] for c in range(n_chunks)], axis=1)
        w = jnp.where(causal, wsp_ref[g], 0.0).astype(_BF16)
        gate = _dot(w, v_cat) + bsp_ref[:, g:g + 1]
        for c in range(n_chunks):
            g_ref[c * CHUNK:(c + 1) * CHUNK, lo:hi] = (
                u[c * CHUNK:(c + 1) * CHUNK] * gate[:, c * CHUNK:(c + 1) * CHUNK]
            ).astype(_BF16)


def _in_projection(x, mod, wqt, wk, wvt, wz, ln_g, ln_b, w_sp, b_sp_t):
    bsz, seq, d = x.shape
    d_attn = wk.shape[1]
    d_z = wz.shape[1]
    rows = PROJ_ROWS
    blk = ATTN_BLOCK
    const = lambda *shape: pl.BlockSpec(shape, lambda b, i: (0,) * len(shape),
                                        pipeline_mode=pl.Buffered(1))
    return pl.pallas_call(
        functools.partial(_proj_kernel, q_scale=ATTN_HEAD_DIM ** -0.5 * math.log2(math.e)),
        grid=(bsz, seq // rows),
        in_specs=[
            pl.BlockSpec((None, rows, d), lambda b, i: (b, i, 0)),
            pl.BlockSpec((None,) + mod.shape[1:], lambda b, i: (b, 0, 0)),
            const(d_attn, d), const(d, d_attn), const(d_attn, d), const(d, d_z),
            const(1, d_z // 2), const(1, d_z // 2),
            const(SGU_GROUPS, CHUNK, CHUNK), const(CHUNK, SGU_GROUPS),
        ],
        out_specs=[
            pl.BlockSpec((None, d_attn, rows), lambda b, i: (b, 0, i)),
            pl.BlockSpec((None, rows, d_attn), lambda b, i: (b, i, 0)),
            pl.BlockSpec((None, rows // blk, d_attn, blk), lambda b, i: (b, i, 0, 0)),
            pl.BlockSpec((None, rows, d_z // 2), lambda b, i: (b, i, 0)),
        ],
        out_shape=[
            jax.ShapeDtypeStruct((bsz, d_attn, seq), _BF16),
            jax.ShapeDtypeStruct((bsz, seq, d_attn), _BF16),
            jax.ShapeDtypeStruct((bsz, seq // blk, d_attn, blk), _BF16),
            jax.ShapeDtypeStruct((bsz, seq, d_z // 2), _BF16),
        ],
        compiler_params=pltpu.CompilerParams(
            dimension_semantics=("parallel", "parallel"),
            vmem_limit_bytes=48 * 1024 * 1024),
        name="in_projection",
    )(x, mod, wqt, wk, wvt, wz, ln_g, ln_b, w_sp, b_sp_t)


def _attn_kernel(lam_ref, qt_ref, k_ref, vt_ref, g_ref, o_ref,
                 qz_ref, s_ref, cmax_ref, m_ref, acc_ref, *, out_scale):
    blk = qt_ref.shape[-1]
    e_dim = ATTN_V_DIM
    i = pl.program_id(1)
    heads = range(ATTN_HEADS)
    cols = [slice(h * e_dim, (h + 1) * e_dim) for h in heads]

    for h in heads:
        qt = qt_ref[cols[h], :]
        row = lax.broadcasted_iota(jnp.int32, qt.shape, 0)
        zero = jnp.zeros_like(qt)
        qz_ref[h, :, :blk] = jnp.where(row < ATTN_HEAD_DIM, qt, zero)
        qz_ref[h, :, blk:] = jnp.where(row >= ATTN_HEAD_DIM, qt, zero)

    m_ref[...] = jnp.full(m_ref.shape, -jnp.inf, _F32)
    acc_ref[...] = jnp.zeros(acc_ref.shape, _F32)
    ones_rows = jnp.ones((acc_ref.shape[1] - e_dim, blk), _BF16)

    def scores(h, kb, masked):
        start = pl.multiple_of(kb * blk, blk)
        s = _dot(k_ref[pl.ds(start, blk), cols[h]], qz_ref[h])
        if masked:
            key = lax.broadcasted_iota(jnp.int32, s.shape, 0)
            qry = lax.broadcasted_iota(jnp.int32, s.shape, 1) % blk
            s = jnp.where(key <= qry, s, -jnp.inf)
        return s

    def stash(h, slot, s):
        s_ref[slot, h] = s
        cmax_ref[slot, h] = jnp.max(s, axis=0, keepdims=True)

    def consume(h, slot, kb):
        m_prev = m_ref[h]
        m_new = jnp.maximum(m_prev, cmax_ref[slot, h])
        alpha = jnp.exp2(m_prev - m_new)
        e = jnp.exp2(s_ref[slot, h] - m_new).astype(_BF16)
        m_ref[h] = m_new
        vt_ones = jnp.concatenate([vt_ref[kb, cols[h], :], ones_rows], axis=0)
        acc_ref[h] = alpha * acc_ref[h] + _dot(vt_ones, e)

    def substep(cur_slot, nxt_slot, cur_blk, nxt_blk):
        ahead = 2
        for h in range(ahead):
            stash(h, nxt_slot, scores(h, nxt_blk, False))
        for h in heads:
            consume(h, cur_slot, cur_blk)
            if h + ahead < ATTN_HEADS:
                stash(h + ahead, nxt_slot, scores(h + ahead, nxt_blk, False))

    for h in heads:
        stash(h, 0, scores(h, i, True))

    def pair(u, carry):
        first = 2 * u
        substep(0, 1, jnp.where(u == 0, i, first - 1), first)
        substep(1, 0, first, first + 1)
        return carry

    lax.fori_loop(0, lax.shift_right_logical(i, 1), pair, 0)

    odd = (i & 1) == 1

    @pl.when(odd)
    def _():
        substep(0, 1, jnp.where(i == 1, i, i - 2), i - 1)
        for h in heads:
            consume(h, 1, i - 1)

    @pl.when(jnp.logical_not(odd))
    def _():
        for h in heads:
            consume(h, 0, jnp.where(i == 0, i, i - 1))

    lam = lam_ref[0]
    for h in heads:
        acc = acc_ref[h]
        o = acc[:e_dim] / acc[e_dim:e_dim + 1]
        o = o[:, :blk] - lam * o[:, blk:]
        ms = jnp.mean(o * o, axis=0, keepdims=True)
        y = o * lax.rsqrt(ms + RMS_EPS) * g_ref[...] * out_scale
        o_ref[:, cols[h]] = y.T.astype(o_ref.dtype)


def _diff_attention(lam, qt, k, vt, subln_g_col, out_scale):
    bsz, d_attn, seq = qt.shape
    blk = ATTN_BLOCK
    e = ATTN_V_DIM
    return pl.pallas_call(
        functools.partial(_attn_kernel, out_scale=out_scale),
        grid=(bsz, seq // blk),
        in_specs=[
            pl.BlockSpec(memory_space=pltpu.SMEM),
            pl.BlockSpec((None, d_attn, blk), lambda b, i: (b, 0, i)),
            pl.BlockSpec((None, seq, d_attn), lambda b, i: (b, 0, 0),
                         pipeline_mode=pl.Buffered(1)),
            pl.BlockSpec((None, seq // blk, d_attn, blk), lambda b, i: (b, 0, 0, 0),
                         pipeline_mode=pl.Buffered(1)),
            pl.BlockSpec((e, 1), lambda b, i: (0, 0)),
        ],
        out_specs=pl.BlockSpec((None, blk, d_attn), lambda b, i: (b, i, 0)),
        out_shape=jax.ShapeDtypeStruct((bsz, seq, d_attn), _BF16),
        scratch_shapes=[
            pltpu.VMEM((ATTN_HEADS, e, 2 * blk), _BF16),
            pltpu.VMEM((2, ATTN_HEADS, blk, 2 * blk), _F32),
            pltpu.VMEM((2, ATTN_HEADS, 1, 2 * blk), _F32),
            pltpu.VMEM((ATTN_HEADS, 1, 2 * blk), _F32),
            pltpu.VMEM((ATTN_HEADS, e + BF16_SUBLANES, 2 * blk), _F32),
        ],
        compiler_params=pltpu.CompilerParams(
            dimension_semantics=("parallel", "arbitrary"),
            vmem_limit_bytes=56 * 1024 * 1024),
        name="diff_attention",
    )(lam, qt, k, vt, subln_g_col)


def _ffn_kernel(x_ref, a_ref, g_ref, mod_ref, wo_a_ref, wo_g_ref, ln1g_ref, ln1b_ref,
                wgate_ref, wup_ref, wdown_ref, ln2g_ref, ln2b_ref, o_ref, *, alpha):
    mod = mod_ref[...]
    mixed = _dot(a_ref[...], wo_a_ref[...]) + _dot(g_ref[...], wo_g_ref[...])
    x1 = _layer_norm_rows(alpha * x_ref[...] + (1.0 + mod[2:3]) * mixed,
                          ln1g_ref[...], ln1b_ref[...])
    h = (x1 * (1.0 + mod[4:5]) + mod[3:4]).astype(_BF16)
    d_ff = wgate_ref.shape[1]
    width = d_ff // FFN_SPLITS
    f = None
    for j in range(FFN_SPLITS):
        cols = slice(j * width, (j + 1) * width)
        gate = _dot(h, wgate_ref[:, cols])
        act = (gate * jax.nn.sigmoid(gate) * _dot(h, wup_ref[:, cols])).astype(_BF16)
        part = _dot(act, wdown_ref[cols, :])
        f = part if f is None else f + part
    o_ref[...] = _layer_norm_rows(alpha * x1 + (1.0 + mod[5:6]) * f,
                                  ln2g_ref[...], ln2b_ref[...])


def _out_ffn(x, a, g_out, mod, wo_a, wo_g, ln1_g, ln1_b, w_gate, w_up, w_down, ln2_g, ln2_b,
             alpha):
    bsz, seq, d = x.shape
    rows = FFN_ROWS
    d_half = a.shape[-1]
    d_ff = w_gate.shape[1]
    const = lambda *shape: pl.BlockSpec(shape, lambda b, i: (0,) * len(shape),
                                        pipeline_mode=pl.Buffered(1))
    return pl.pallas_call(
        functools.partial(_ffn_kernel, alpha=alpha),
        grid=(bsz, seq // rows),
        in_specs=[
            pl.BlockSpec((None, rows, d), lambda b, i: (b, i, 0)),
            pl.BlockSpec((None, rows, d_half), lambda b, i: (b, i, 0)),
            pl.BlockSpec((None, rows, d_half), lambda b, i: (b, i, 0)),
            pl.BlockSpec((None,) + mod.shape[1:], lambda b, i: (b, 0, 0)),
            const(d_half, d), const(d_half, d), const(1, d), const(1, d),
            const(d, d_ff), const(d, d_ff), const(d_ff, d), const(1, d), const(1, d),
        ],
        out_specs=pl.BlockSpec((None, rows, d), lambda b, i: (b, i, 0)),
        out_shape=jax.ShapeDtypeStruct((bsz, seq, d), x.dtype),
        compiler_params=pltpu.CompilerParams(
            dimension_semantics=("parallel", "parallel"),
            vmem_limit_bytes=56 * 1024 * 1024),
        name="out_ffn",
    )(x, a, g_out, mod, wo_a, wo_g, ln1_g, ln1_b, w_gate, w_up, w_down, ln2_g, ln2_b)


def kernel(x, c, w_ada, b_ada, w_in, lambda_q1, lambda_k1, lambda_q2, lambda_k2, subln_g,
           sgu_ln_g, sgu_ln_b, w_spatial, b_spatial, w_out, ln1_g, ln1_b, w_gate, w_up, w_down,
           ln2_g, ln2_b):
    depth = w_in.shape[0]
    d = x.shape[-1]
    d_attn = ATTN_HEADS * ATTN_V_DIM
    alpha = (2 * depth) ** 0.25
    for l in range(depth):
        lam_init = 0.8 - 0.6 * math.exp(-0.3 * l)
        lam_vecs = jnp.stack([lambda_q1[l], lambda_k1[l], lambda_q2[l], lambda_k2[l]])
        mod, lam = _adaln_mod(c, w_ada[l], b_ada[l], lam_vecs.astype(_F32), lam_init)

        w = w_in[l].astype(_BF16)
        wqt = w[:, :d_attn].T
        wk = w[:, d_attn:2 * d_attn]
        wvt = w[:, 2 * d_attn:3 * d_attn].T
        wz = w[:, 3 * d_attn:]
        qt, k, vt, g_out = _in_projection(
            x, mod, wqt, wk, wvt, wz,
            sgu_ln_g[l].reshape(1, -1), sgu_ln_b[l].reshape(1, -1),
            w_spatial[l], b_spatial[l].T)

        a = _diff_attention(lam, qt, k, vt, subln_g[l].reshape(-1, 1), 1.0 - lam_init)

        wo = w_out[l].astype(_BF16)
        x = _out_ffn(
            x, a, g_out, mod, wo[:d_attn], wo[d_attn:],
            ln1_g[l].reshape(1, d), ln1_b[l].reshape(1, d),
            w_gate[l].astype(_BF16), w_up[l].astype(_BF16), w_down[l].astype(_BF16),
            ln2_g[l].reshape(1, d), ln2_b[l].reshape(1, d), alpha)
    return x
```

```python
import functools
import math

import jax
import jax.numpy as jnp
from jax import lax
from jax.experimental import pallas as pl
from jax.experimental.pallas import tpu as pltpu

ATTN_HEADS = 4
ATTN_HEAD_DIM = 64
ATTN_V_DIM = 2 * ATTN_HEAD_DIM
SGU_GROUPS = 4
SGU_GROUP_DIM = 128
CHUNK = 128
LN_EPS = 1e-5
RMS_EPS = 1e-5

SUBLANES = 8
BF16_SUBLANES = 16
LANES = 128

PROJ_ROWS = 1024
PROJ_SUBROWS = 256
ATTN_BLOCK = 512
FFN_ROWS = 1024
FFN_SUBROWS = 256
FFN_COLS = 1024

_BF16 = jnp.bfloat16
_F32 = jnp.float32


def _dot(a, b):
    return jnp.dot(a, b, preferred_element_type=_F32)


def _dot_nt(a, b):
    return lax.dot_general(a, b, (((1,), (1,)), ((), ())), preferred_element_type=_F32)


def _layer_norm_rows(x, g, b):
    mu = jnp.mean(x, axis=-1, keepdims=True)
    xc = x - mu
    var = jnp.mean(xc * xc, axis=-1, keepdims=True)
    return xc * lax.rsqrt(var + LN_EPS) * g + b


def _mod_kernel(c_ref, w_ref, b_ref, lam_vec_ref, mod_ref, lam_ref, *, lam_init):
    c = c_ref[...]
    c_act = c * jax.nn.sigmoid(c)
    mod_ref[...] = jnp.dot(c_act, w_ref[...], preferred_element_type=_F32,
                           precision=lax.Precision.HIGHEST) + b_ref[...]
    lv = lam_vec_ref[...]
    s1 = jnp.sum(lv[0:1] * lv[1:2], axis=-1, keepdims=True)
    s2 = jnp.sum(lv[2:3] * lv[3:4], axis=-1, keepdims=True)
    lam = jnp.exp(s1) - jnp.exp(s2) + lam_init
    lam_ref[...] = jnp.broadcast_to(lam, lam_ref.shape)


def _adaln_mod(c, w_ada, b_ada, lam_vecs, lam_init):
    bsz, d = c.shape
    n = w_ada.shape[1]
    c_pad = jnp.zeros((SUBLANES, d), _F32).at[:bsz].set(c)
    mod, lam = pl.pallas_call(
        functools.partial(_mod_kernel, lam_init=lam_init),
        grid=(n // d,),
        in_specs=[
            pl.BlockSpec((SUBLANES, d), lambda j: (0, 0)),
            pl.BlockSpec((d, d), lambda j: (0, j)),
            pl.BlockSpec((1, d), lambda j: (0, j)),
            pl.BlockSpec(lam_vecs.shape, lambda j: (0, 0)),
        ],
        out_specs=[
            pl.BlockSpec((SUBLANES, d), lambda j: (0, j)),
            pl.BlockSpec((SUBLANES, LANES), lambda j: (0, 0)),
        ],
        out_shape=[
            jax.ShapeDtypeStruct((SUBLANES, n), _F32),
            jax.ShapeDtypeStruct((SUBLANES, LANES), _F32),
        ],
        name="adaln_mod",
    )(c_pad, w_ada, b_ada.reshape(1, n), lam_vecs)
    return mod[:bsz].reshape(bsz, n // d, d), lam[0, :1]


def _proj_kernel(x_ref, mod_ref, wqt_ref, wk_ref, wvt_ref, wz_ref, lng_ref, lnb_ref,
                 wsp_ref, bsp_ref, qt_ref, k_ref, vt_ref, g_ref, *, q_scale):
    rows = x_ref.shape[0]
    blk = vt_ref.shape[-1]
    mod = mod_ref[...]
    starts = list(range(0, rows, PROJ_SUBROWS))
    h = [(x_ref[r:r + PROJ_SUBROWS, :] * (1.0 + mod[1:2]) + mod[0:1]).astype(_BF16)
         for r in starts]
    z = [_dot(v, wz_ref[...]) for v in h]

    d_sgu = SGU_GROUPS * SGU_GROUP_DIM
    n_chunks = PROJ_SUBROWS // CHUNK
    t_idx = lax.broadcasted_iota(jnp.int32, (CHUNK, CHUNK), 0)
    s_idx = lax.broadcasted_iota(jnp.int32, (CHUNK, CHUNK), 1)
    causal = s_idx <= t_idx
    w_sp = [jnp.where(causal, wsp_ref[g], 0.0).astype(_BF16) for g in range(SGU_GROUPS)]
    for r, v, zr in zip(starts, h, z):
        k_ref[r:r + PROJ_SUBROWS, :] = _dot(v, wk_ref[...]).astype(_BF16)
        qt_ref[:, r:r + PROJ_SUBROWS] = (_dot_nt(wqt_ref[...], v) * q_scale).astype(_BF16)
        lane = r % blk
        vt_ref[r // blk, :, lane:lane + PROJ_SUBROWS] = _dot_nt(wvt_ref[...], v).astype(_BF16)
        zr = jax.nn.gelu(zr, approximate=True)
        for g in range(SGU_GROUPS):
            lo = g * SGU_GROUP_DIM
            hi = lo + SGU_GROUP_DIM
            u = zr[:, lo:hi]
            vs = zr[:, d_sgu + lo:d_sgu + hi]
            vn = _layer_norm_rows(vs, lng_ref[:, lo:hi], lnb_ref[:, lo:hi]).astype(_BF16)
            v_cat = jnp.concatenate(
                [vn[c * CHUNK:(c + 1) * CHUNK] for c in range(n_chunks)], axis=1)
            gate = _dot(w_sp[g], v_cat) + bsp_ref[:, g:g + 1]
            for c in range(n_chunks):
                g_ref[r + c * CHUNK:r + (c + 1) * CHUNK, lo:hi] = (
                    u[c * CHUNK:(c + 1) * CHUNK] * gate[:, c * CHUNK:(c + 1) * CHUNK]
                ).astype(_BF16)


def _in_projection(x, mod, wqt, wk, wvt, wz, ln_g, ln_b, w_sp, b_sp_t):
    bsz, seq, d = x.shape
    d_attn = wk.shape[1]
    d_z = wz.shape[1]
    rows = PROJ_ROWS
    blk = ATTN_BLOCK
    const = lambda *shape: pl.BlockSpec(shape, lambda b, i: (0,) * len(shape),
                                        pipeline_mode=pl.Buffered(1))
    return pl.pallas_call(
        functools.partial(_proj_kernel, q_scale=ATTN_HEAD_DIM ** -0.5 * math.log2(math.e)),
        grid=(bsz, seq // rows),
        in_specs=[
            pl.BlockSpec((None, rows, d), lambda b, i: (b, i, 0)),
            pl.BlockSpec((None,) + mod.shape[1:], lambda b, i: (b, 0, 0)),
            const(d_attn, d), const(d, d_attn), const(d_attn, d), const(d, d_z),
            const(1, d_z // 2), const(1, d_z // 2),
            const(SGU_GROUPS, CHUNK, CHUNK), const(CHUNK, SGU_GROUPS),
        ],
        out_specs=[
            pl.BlockSpec((None, d_attn, rows), lambda b, i: (b, 0, i)),
            pl.BlockSpec((None, rows, d_attn), lambda b, i: (b, i, 0)),
            pl.BlockSpec((None, rows // blk, d_attn, blk), lambda b, i: (b, i, 0, 0)),
            pl.BlockSpec((None, rows, d_z // 2), lambda b, i: (b, i, 0)),
        ],
        out_shape=[
            jax.ShapeDtypeStruct((bsz, d_attn, seq), _BF16),
            jax.ShapeDtypeStruct((bsz, seq, d_attn), _BF16),
            jax.ShapeDtypeStruct((bsz, seq // blk, d_attn, blk), _BF16),
            jax.ShapeDtypeStruct((bsz, seq, d_z // 2), _BF16),
        ],
        compiler_params=pltpu.CompilerParams(
            dimension_semantics=("parallel", "parallel"),
            vmem_limit_bytes=48 * 1024 * 1024),
        name="in_projection",
    )(x, mod, wqt, wk, wvt, wz, ln_g, ln_b, w_sp, b_sp_t)


def _attn_kernel(lam_ref, qt_ref, k_ref, vt_ref, g_ref, o_ref,
                 qz_ref, s_ref, cmax_ref, m_ref, acc_ref, *, out_scale):
    blk = qt_ref.shape[-1]
    e_dim = ATTN_V_DIM
    i = pl.program_id(1)
    heads = range(ATTN_HEADS)
    cols = [slice(h * e_dim, (h + 1) * e_dim) for h in heads]

    for h in heads:
        qt = qt_ref[cols[h], :]
        row = lax.broadcasted_iota(jnp.int32, qt.shape, 0)
        zero = jnp.zeros_like(qt)
        qz_ref[h, :, :blk] = jnp.where(row < ATTN_HEAD_DIM, qt, zero)
        qz_ref[h, :, blk:] = jnp.where(row >= ATTN_HEAD_DIM, qt, zero)

    m_ref[...] = jnp.full(m_ref.shape, -jnp.inf, _F32)
    acc_ref[...] = jnp.zeros(acc_ref.shape, _F32)
    ones_rows = jnp.ones((acc_ref.shape[1] - e_dim, blk), _BF16)

    def scores(h, kb, masked):
        start = pl.multiple_of(kb * blk, blk)
        s = _dot(k_ref[pl.ds(start, blk), cols[h]], qz_ref[h])
        if masked:
            key = lax.broadcasted_iota(jnp.int32, s.shape, 0)
            qry = lax.broadcasted_iota(jnp.int32, s.shape, 1) % blk
            s = jnp.where(key <= qry, s, -jnp.inf)
        return s

    def stash(h, slot, s):
        s_ref[slot, h] = s
        cmax_ref[slot, h] = jnp.max(s, axis=0, keepdims=True)

    def consume(h, slot, kb):
        m_prev = m_ref[h]
        m_new = jnp.maximum(m_prev, cmax_ref[slot, h])
        alpha = jnp.exp2(m_prev - m_new)
        e = jnp.exp2(s_ref[slot, h] - m_new).astype(_BF16)
        m_ref[h] = m_new
        vt_ones = jnp.concatenate([vt_ref[kb, cols[h], :], ones_rows], axis=0)
        acc_ref[h] = alpha * acc_ref[h] + _dot(vt_ones, e)

    def substep(cur_slot, nxt_slot, cur_blk, nxt_blk):
        ahead = 2
        for h in range(ahead):
            stash(h, nxt_slot, scores(h, nxt_blk, False))
        for h in heads:
            consume(h, cur_slot, cur_blk)
            if h + ahead < ATTN_HEADS:
                stash(h + ahead, nxt_slot, scores(h + ahead, nxt_blk, False))

    for h in heads:
        stash(h, 0, scores(h, i, True))

    def pair(u, carry):
        first = 2 * u
        substep(0, 1, jnp.where(u == 0, i, first - 1), first)
        substep(1, 0, first, first + 1)
        return carry

    lax.fori_loop(0, lax.shift_right_logical(i, 1), pair, 0)

    odd = (i & 1) == 1

    @pl.when(odd)
    def _():
        substep(0, 1, jnp.where(i == 1, i, i - 2), i - 1)
        for h in heads:
            consume(h, 1, i - 1)

    @pl.when(jnp.logical_not(odd))
    def _():
        for h in heads:
            consume(h, 0, jnp.where(i == 0, i, i - 1))

    lam = lam_ref[0]
    for h in heads:
        acc = acc_ref[h]
        o = acc[:e_dim] / acc[e_dim:e_dim + 1]
        o = o[:, :blk] - lam * o[:, blk:]
        ms = jnp.mean(o * o, axis=0, keepdims=True)
        y = o * lax.rsqrt(ms + RMS_EPS) * g_ref[...] * out_scale
        o_ref[:, cols[h]] = y.T.astype(o_ref.dtype)


def _diff_attention(lam, qt, k, vt, subln_g_col, out_scale):
    bsz, d_attn, seq = qt.shape
    blk = ATTN_BLOCK
    e = ATTN_V_DIM
    return pl.pallas_call(
        functools.partial(_attn_kernel, out_scale=out_scale),
        grid=(bsz, seq // blk),
        in_specs=[
            pl.BlockSpec(memory_space=pltpu.SMEM),
            pl.BlockSpec((None, d_attn, blk), lambda b, i: (b, 0, i)),
            pl.BlockSpec((None, seq, d_attn), lambda b, i: (b, 0, 0),
                         pipeline_mode=pl.Buffered(1)),
            pl.BlockSpec((None, seq // blk, d_attn, blk), lambda b, i: (b, 0, 0, 0),
                         pipeline_mode=pl.Buffered(1)),
            pl.BlockSpec((e, 1), lambda b, i: (0, 0)),
        ],
        out_specs=pl.BlockSpec((None, blk, d_attn), lambda b, i: (b, i, 0)),
        out_shape=jax.ShapeDtypeStruct((bsz, seq, d_attn), _BF16),
        scratch_shapes=[
            pltpu.VMEM((ATTN_HEADS, e, 2 * blk), _BF16),
            pltpu.VMEM((2, ATTN_HEADS, blk, 2 * blk), _F32),
            pltpu.VMEM((2, ATTN_HEADS, 1, 2 * blk), _F32),
            pltpu.VMEM((ATTN_HEADS, 1, 2 * blk), _F32),
            pltpu.VMEM((ATTN_HEADS, e + BF16_SUBLANES, 2 * blk), _F32),
        ],
        compiler_params=pltpu.CompilerParams(
            dimension_semantics=("parallel", "arbitrary"),
            vmem_limit_bytes=56 * 1024 * 1024),
        name="diff_attention",
    )(lam, qt, k, vt, subln_g_col)


def _ffn_kernel(x_ref, a_ref, g_ref, mod_ref, wo_a_ref, wo_g_ref, ln1g_ref, ln1b_ref,
                wgate_ref, wup_ref, wdown_ref, ln2g_ref, ln2b_ref, o_ref, *, alpha):
    mod = mod_ref[...]
    rows = x_ref.shape[0]
    groups = [slice(r, r + FFN_SUBROWS) for r in range(0, rows, FFN_SUBROWS)]
    mixed = [_dot(a_ref[r, :], wo_a_ref[...]) + _dot(g_ref[r, :], wo_g_ref[...]) for r in groups]
    x1 = [_layer_norm_rows(alpha * x_ref[r, :] + (1.0 + mod[2:3]) * mx,
                           ln1g_ref[...], ln1b_ref[...]) for r, mx in zip(groups, mixed)]
    h = [(v * (1.0 + mod[4:5]) + mod[3:4]).astype(_BF16) for v in x1]
    d_ff = wgate_ref.shape[1]
    f = [None] * len(groups)
    for lo in range(0, d_ff, FFN_COLS):
        cols = slice(lo, min(lo + FFN_COLS, d_ff))
        gate = [_dot(v, wgate_ref[:, cols]) for v in h]
        up = [_dot(v, wup_ref[:, cols]) for v in h]
        act = [(gt * jax.nn.sigmoid(gt) * u).astype(_BF16) for gt, u in zip(gate, up)]
        part = [_dot(v, wdown_ref[cols, :]) for v in act]
        f = [p if acc is None else acc + p for acc, p in zip(f, part)]
    for r, v, fv in zip(groups, x1, f):
        o_ref[r, :] = _layer_norm_rows(alpha * v + (1.0 + mod[5:6]) * fv,
                                       ln2g_ref[...], ln2b_ref[...])


def _out_ffn(x, a, g_out, mod, wo_a, wo_g, ln1_g, ln1_b, w_gate, w_up, w_down, ln2_g, ln2_b,
             alpha):
    bsz, seq, d = x.shape
    rows = FFN_ROWS
    d_half = a.shape[-1]
    d_ff = w_gate.shape[1]
    const = lambda *shape: pl.BlockSpec(shape, lambda b, i: (0,) * len(shape),
                                        pipeline_mode=pl.Buffered(1))
    return pl.pallas_call(
        functools.partial(_ffn_kernel, alpha=alpha),
        grid=(bsz, seq // rows),
        in_specs=[
            pl.BlockSpec((None, rows, d), lambda b, i: (b, i, 0)),
            pl.BlockSpec((None, rows, d_half), lambda b, i: (b, i, 0)),
            pl.BlockSpec((None, rows, d_half), lambda b, i: (b, i, 0)),
            pl.BlockSpec((None,) + mod.shape[1:], lambda b, i: (b, 0, 0)),
            const(d_half, d), const(d_half, d), const(1, d), const(1, d),
            const(d, d_ff), const(d, d_ff), const(d_ff, d), const(1, d), const(1, d),
        ],
        out_specs=pl.BlockSpec((None, rows, d), lambda b, i: (b, i, 0)),
        out_shape=jax.ShapeDtypeStruct((bsz, seq, d), x.dtype),
        compiler_params=pltpu.CompilerParams(
            dimension_semantics=("parallel", "parallel"),
            vmem_limit_bytes=56 * 1024 * 1024),
        name="out_ffn",
    )(x, a, g_out, mod, wo_a, wo_g, ln1_g, ln1_b, w_gate, w_up, w_down, ln2_g, ln2_b)


def kernel(x, c, w_ada, b_ada, w_in, lambda_q1, lambda_k1, lambda_q2, lambda_k2, subln_g,
           sgu_ln_g, sgu_ln_b, w_spatial, b_spatial, w_out, ln1_g, ln1_b, w_gate, w_up, w_down,
           ln2_g, ln2_b):
    depth = w_in.shape[0]
    d = x.shape[-1]
    d_attn = ATTN_HEADS * ATTN_V_DIM
    alpha = (2 * depth) ** 0.25
    for l in range(depth):
        lam_init = 0.8 - 0.6 * math.exp(-0.3 * l)
        lam_vecs = jnp.stack([lambda_q1[l], lambda_k1[l], lambda_q2[l], lambda_k2[l]])
        mod, lam = _adaln_mod(c, w_ada[l], b_ada[l], lam_vecs.astype(_F32), lam_init)

        w = w_in[l].astype(_BF16)
        wqt = w[:, :d_attn].T
        wk = w[:, d_attn:2 * d_attn]
        wvt = w[:, 2 * d_attn:3 * d_attn].T
        wz = w[:, 3 * d_attn:]
        qt, k, vt, g_out = _in_projection(
            x, mod, wqt, wk, wvt, wz,
            sgu_ln_g[l].reshape(1, -1), sgu_ln_b[l].reshape(1, -1),
            w_spatial[l], b_spatial[l].T)

        a = _diff_attention(lam, qt, k, vt, subln_g[l].reshape(-1, 1), 1.0 - lam_init)

        wo = w_out[l].astype(_BF16)
        x = _out_ffn(
            x, a, g_out, mod, wo[:d_attn], wo[d_attn:],
            ln1_g[l].reshape(1, d), ln1_b[l].reshape(1, d),
            w_gate[l].astype(_BF16), w_up[l].astype(_BF16), w_down[l].astype(_BF16),
            ln2_g[l].reshape(1, d), ln2_b[l].reshape(1, d), alpha)
    return x
```

```python
import functools
import math

import jax
import jax.numpy as jnp
from jax import lax
from jax.experimental import pallas as pl
from jax.experimental.pallas import tpu as pltpu

ATTN_HEADS = 4
ATTN_HEAD_DIM = 64
ATTN_V_DIM = 2 * ATTN_HEAD_DIM
SGU_GROUPS = 4
SGU_GROUP_DIM = 128
CHUNK = 128
LN_EPS = 1e-5
RMS_EPS = 1e-5

SUBLANES = 8
BF16_SUBLANES = 16
LANES = 128

PROJ_ROWS = 1024
PROJ_SUBROWS = 256
ATTN_BLOCK = 512
ATTN_LEAD = 2
FFN_ROWS = 1024
FFN_SUBROWS = 256
FFN_COLS = 1024

_BF16 = jnp.bfloat16
_F32 = jnp.float32


def _dot(a, b):
    return jnp.dot(a, b, preferred_element_type=_F32)


def _dot_nt(a, b):
    return lax.dot_general(a, b, (((1,), (1,)), ((), ())), preferred_element_type=_F32)


def _layer_norm_rows(x, g, b):
    mu = jnp.mean(x, axis=-1, keepdims=True)
    xc = x - mu
    var = jnp.mean(xc * xc, axis=-1, keepdims=True)
    return xc * lax.rsqrt(var + LN_EPS) * g + b


def _mod_kernel(c_ref, w_ref, b_ref, lam_vec_ref, mod_ref, lam_ref, *, lam_init):
    c = c_ref[...]
    c_act = c * jax.nn.sigmoid(c)
    mod_ref[...] = jnp.dot(c_act, w_ref[...], preferred_element_type=_F32,
                           precision=lax.Precision.HIGHEST) + b_ref[...]
    lv = lam_vec_ref[...]
    s1 = jnp.sum(lv[0:1] * lv[1:2], axis=-1, keepdims=True)
    s2 = jnp.sum(lv[2:3] * lv[3:4], axis=-1, keepdims=True)
    lam = jnp.exp(s1) - jnp.exp(s2) + lam_init
    lam_ref[...] = jnp.broadcast_to(lam, lam_ref.shape)


def _adaln_mod(c, w_ada, b_ada, lam_vecs, lam_init):
    bsz, d = c.shape
    n = w_ada.shape[1]
    c_pad = jnp.zeros((SUBLANES, d), _F32).at[:bsz].set(c)
    mod, lam = pl.pallas_call(
        functools.partial(_mod_kernel, lam_init=lam_init),
        grid=(n // d,),
        in_specs=[
            pl.BlockSpec((SUBLANES, d), lambda j: (0, 0)),
            pl.BlockSpec((d, d), lambda j: (0, j)),
            pl.BlockSpec((1, d), lambda j: (0, j)),
            pl.BlockSpec(lam_vecs.shape, lambda j: (0, 0)),
        ],
        out_specs=[
            pl.BlockSpec((SUBLANES, d), lambda j: (0, j)),
            pl.BlockSpec((SUBLANES, LANES), lambda j: (0, 0)),
        ],
        out_shape=[
            jax.ShapeDtypeStruct((SUBLANES, n), _F32),
            jax.ShapeDtypeStruct((SUBLANES, LANES), _F32),
        ],
        name="adaln_mod",
    )(c_pad, w_ada, b_ada.reshape(1, n), lam_vecs)
    return mod[:bsz].reshape(bsz, n // d, d), lam[0, :1]


def _proj_kernel(x_ref, mod_ref, wqt_ref, wk_ref, wvt_ref, wz_ref, lng_ref, lnb_ref,
                 wsp_ref, bsp_ref, qt_ref, k_ref, vt_ref, g_ref, *, q_scale):
    rows = x_ref.shape[0]
    blk = vt_ref.shape[-1]
    mod = mod_ref[...]
    starts = list(range(0, rows, PROJ_SUBROWS))
    h = [(x_ref[r:r + PROJ_SUBROWS, :] * (1.0 + mod[1:2]) + mod[0:1]).astype(_BF16)
         for r in starts]
    z = [_dot(v, wz_ref[...]) for v in h]

    d_sgu = SGU_GROUPS * SGU_GROUP_DIM
    n_chunks = PROJ_SUBROWS // CHUNK
    t_idx = lax.broadcasted_iota(jnp.int32, (CHUNK, CHUNK), 0)
    s_idx = lax.broadcasted_iota(jnp.int32, (CHUNK, CHUNK), 1)
    causal = s_idx <= t_idx
    w_sp = [jnp.where(causal, wsp_ref[g], 0.0).astype(_BF16) for g in range(SGU_GROUPS)]
    for r, v, zr in zip(starts, h, z):
        k_ref[r:r + PROJ_SUBROWS, :] = _dot(v, wk_ref[...]).astype(_BF16)
        qt_ref[:, r:r + PROJ_SUBROWS] = (_dot_nt(wqt_ref[...], v) * q_scale).astype(_BF16)
        lane = r % blk
        vt_ref[r // blk, :, lane:lane + PROJ_SUBROWS] = _dot_nt(wvt_ref[...], v).astype(_BF16)
        zr = jax.nn.gelu(zr, approximate=True)
        for g in range(SGU_GROUPS):
            lo = g * SGU_GROUP_DIM
            hi = lo + SGU_GROUP_DIM
            u = zr[:, lo:hi]
            vs = zr[:, d_sgu + lo:d_sgu + hi]
            vn = _layer_norm_rows(vs, lng_ref[:, lo:hi], lnb_ref[:, lo:hi]).astype(_BF16)
            v_cat = jnp.concatenate(
                [vn[c * CHUNK:(c + 1) * CHUNK] for c in range(n_chunks)], axis=1)
            gate = _dot(w_sp[g], v_cat) + bsp_ref[:, g:g + 1]
            for c in range(n_chunks):
                g_ref[r + c * CHUNK:r + (c + 1) * CHUNK, lo:hi] = (
                    u[c * CHUNK:(c + 1) * CHUNK] * gate[:, c * CHUNK:(c + 1) * CHUNK]
                ).astype(_BF16)


def _in_projection(x, mod, wqt, wk, wvt, wz, ln_g, ln_b, w_sp, b_sp_t):
    bsz, seq, d = x.shape
    d_attn = wk.shape[1]
    d_z = wz.shape[1]
    rows = PROJ_ROWS
    blk = ATTN_BLOCK
    const = lambda *shape: pl.BlockSpec(shape, lambda b, i: (0,) * len(shape),
                                        pipeline_mode=pl.Buffered(1))
    return pl.pallas_call(
        functools.partial(_proj_kernel, q_scale=ATTN_HEAD_DIM ** -0.5 * math.log2(math.e)),
        grid=(bsz, seq // rows),
        in_specs=[
            pl.BlockSpec((None, rows, d), lambda b, i: (b, i, 0)),
            pl.BlockSpec((None,) + mod.shape[1:], lambda b, i: (b, 0, 0)),
            const(d_attn, d), const(d, d_attn), const(d_attn, d), const(d, d_z),
            const(1, d_z // 2), const(1, d_z // 2),
            const(SGU_GROUPS, CHUNK, CHUNK), const(CHUNK, SGU_GROUPS),
        ],
        out_specs=[
            pl.BlockSpec((None, d_attn, rows), lambda b, i: (b, 0, i)),
            pl.BlockSpec((None, rows, d_attn), lambda b, i: (b, i, 0)),
            pl.BlockSpec((None, rows // blk, d_attn, blk), lambda b, i: (b, i, 0, 0)),
            pl.BlockSpec((None, rows, d_z // 2), lambda b, i: (b, i, 0)),
        ],
        out_shape=[
            jax.ShapeDtypeStruct((bsz, d_attn, seq), _BF16),
            jax.ShapeDtypeStruct((bsz, seq, d_attn), _BF16),
            jax.ShapeDtypeStruct((bsz, seq // blk, d_attn, blk), _BF16),
            jax.ShapeDtypeStruct((bsz, seq, d_z // 2), _BF16),
        ],
        compiler_params=pltpu.CompilerParams(
            dimension_semantics=("parallel", "parallel"),
            vmem_limit_bytes=48 * 1024 * 1024),
        name="in_projection",
    )(x, mod, wqt, wk, wvt, wz, ln_g, ln_b, w_sp, b_sp_t)


def _attn_kernel(lam_ref, qt_ref, k_ref, vt_ref, g_ref, o_ref,
                 qz_ref, s_ref, cmax_ref, m_ref, acc_ref, *, out_scale):
    blk = qt_ref.shape[-1]
    e_dim = ATTN_V_DIM
    i = pl.program_id(1)
    heads = range(ATTN_HEADS)
    cols = [slice(h * e_dim, (h + 1) * e_dim) for h in heads]

    for h in heads:
        qt = qt_ref[cols[h], :]
        row = lax.broadcasted_iota(jnp.int32, qt.shape, 0)
        zero = jnp.zeros_like(qt)
        qz_ref[h, :, :blk] = jnp.where(row < ATTN_HEAD_DIM, qt, zero)
        qz_ref[h, :, blk:] = jnp.where(row >= ATTN_HEAD_DIM, qt, zero)

    m_ref[...] = jnp.full(m_ref.shape, -jnp.inf, _F32)
    acc_ref[...] = jnp.zeros(acc_ref.shape, _F32)
    ones_rows = jnp.ones((acc_ref.shape[1] - e_dim, blk), _BF16)

    def produce(h, kb, masked):
        start = pl.multiple_of(kb * blk, blk)
        s = _dot(k_ref[pl.ds(start, blk), cols[h]], qz_ref[h])
        if masked:
            key = lax.broadcasted_iota(jnp.int32, s.shape, 0)
            qry = lax.broadcasted_iota(jnp.int32, s.shape, 1) % blk
            s = jnp.where(key <= qry, s, -jnp.inf)
        s_ref[h] = s
        cmax_ref[h] = jnp.max(s, axis=0, keepdims=True)

    def consume(h, kb):
        m_prev = m_ref[h]
        m_new = jnp.maximum(m_prev, cmax_ref[h])
        alpha = jnp.exp2(m_prev - m_new)
        e = jnp.exp2(s_ref[h] - m_new).astype(_BF16)
        m_ref[h] = m_new
        vt_ones = jnp.concatenate([vt_ref[kb, cols[h], :], ones_rows], axis=0)
        acc_ref[h] = alpha * acc_ref[h] + _dot(vt_ones, e)

    def block_steps(kb, masked, next_masked):
        for h in heads:
            nxt = h + ATTN_LEAD
            if nxt < ATTN_HEADS:
                produce(nxt, kb, masked)
            elif next_masked is not None:
                produce(nxt - ATTN_HEADS, kb + 1, next_masked)
            consume(h, kb)

    @pl.when(i == 0)
    def _():
        for h in range(ATTN_LEAD):
            produce(h, 0, True)

    @pl.when(i > 0)
    def _():
        for h in range(ATTN_LEAD):
            produce(h, 0, False)

    n_plain = jnp.maximum(i - 1, 0)

    def pair(u, carry):
        block_steps(2 * u, False, False)
        block_steps(2 * u + 1, False, False)
        return carry

    lax.fori_loop(0, lax.shift_right_logical(n_plain, 1), pair, 0)

    @pl.when((n_plain & 1) == 1)
    def _():
        block_steps(i - 2, False, False)

    @pl.when(i > 0)
    def _():
        block_steps(i - 1, False, True)

    block_steps(i, True, None)

    lam = lam_ref[0]
    for h in heads:
        acc = acc_ref[h]
        o = acc[:e_dim] / acc[e_dim:e_dim + 1]
        o = o[:, :blk] - lam * o[:, blk:]
        ms = jnp.mean(o * o, axis=0, keepdims=True)
        y = o * lax.rsqrt(ms + RMS_EPS) * g_ref[...] * out_scale
        o_ref[:, cols[h]] = y.T.astype(o_ref.dtype)


def _diff_attention(lam, qt, k, vt, subln_g_col, out_scale):
    bsz, d_attn, seq = qt.shape
    blk = ATTN_BLOCK
    e = ATTN_V_DIM
    return pl.pallas_call(
        functools.partial(_attn_kernel, out_scale=out_scale),
        grid=(bsz, seq // blk),
        in_specs=[
            pl.BlockSpec(memory_space=pltpu.SMEM),
            pl.BlockSpec((None, d_attn, blk), lambda b, i: (b, 0, i)),
            pl.BlockSpec((None, seq, d_attn), lambda b, i: (b, 0, 0)),
            pl.BlockSpec((None, seq // blk, d_attn, blk), lambda b, i: (b, 0, 0, 0)),
            pl.BlockSpec((e, 1), lambda b, i: (0, 0)),
        ],
        out_specs=pl.BlockSpec((None, blk, d_attn), lambda b, i: (b, i, 0)),
        out_shape=jax.ShapeDtypeStruct((bsz, seq, d_attn), _BF16),
        scratch_shapes=[
            pltpu.VMEM((ATTN_HEADS, e, 2 * blk), _BF16),
            pltpu.VMEM((ATTN_HEADS, blk, 2 * blk), _F32),
            pltpu.VMEM((ATTN_HEADS, 1, 2 * blk), _F32),
            pltpu.VMEM((ATTN_HEADS, 1, 2 * blk), _F32),
            pltpu.VMEM((ATTN_HEADS, e + BF16_SUBLANES, 2 * blk), _F32),
        ],
        compiler_params=pltpu.CompilerParams(
            dimension_semantics=("parallel", "arbitrary"),
            vmem_limit_bytes=56 * 1024 * 1024),
        name="diff_attention",
    )(lam, qt, k, vt, subln_g_col)


def _ffn_kernel(x_ref, a_ref, g_ref, mod_ref, wo_a_ref, wo_g_ref, ln1g_ref, ln1b_ref,
                wgate_ref, wup_ref, wdown_ref, ln2g_ref, ln2b_ref, o_ref, *, alpha):
    mod = mod_ref[...]
    rows = x_ref.shape[0]
    groups = [slice(r, r + FFN_SUBROWS) for r in range(0, rows, FFN_SUBROWS)]
    mixed = [_dot(a_ref[r, :], wo_a_ref[...]) + _dot(g_ref[r, :], wo_g_ref[...]) for r in groups]
    x1 = [_layer_norm_rows(alpha * x_ref[r, :] + (1.0 + mod[2:3]) * mx,
                           ln1g_ref[...], ln1b_ref[...]) for r, mx in zip(groups, mixed)]
    h = [(v * (1.0 + mod[4:5]) + mod[3:4]).astype(_BF16) for v in x1]
    d_ff = wgate_ref.shape[1]
    f = [None] * len(groups)
    for lo in range(0, d_ff, FFN_COLS):
        cols = slice(lo, min(lo + FFN_COLS, d_ff))
        gate = [_dot(v, wgate_ref[:, cols]) for v in h]
        up = [_dot(v, wup_ref[:, cols]) for v in h]
        act = [(gt * jax.nn.sigmoid(gt) * u).astype(_BF16) for gt, u in zip(gate, up)]
        part = [_dot(v, wdown_ref[cols, :]) for v in act]
        f = [p if acc is None else acc + p for acc, p in zip(f, part)]
    for r, v, fv in zip(groups, x1, f):
        o_ref[r, :] = _layer_norm_rows(alpha * v + (1.0 + mod[5:6]) * fv,
                                       ln2g_ref[...], ln2b_ref[...])


def _out_ffn(x, a, g_out, mod, wo_a, wo_g, ln1_g, ln1_b, w_gate, w_up, w_down, ln2_g, ln2_b,
             alpha):
    bsz, seq, d = x.shape
    rows = FFN_ROWS
    d_half = a.shape[-1]
    d_ff = w_gate.shape[1]
    const = lambda *shape: pl.BlockSpec(shape, lambda b, i: (0,) * len(shape),
                                        pipeline_mode=pl.Buffered(1))
    return pl.pallas_call(
        functools.partial(_ffn_kernel, alpha=alpha),
        grid=(bsz, seq // rows),
        in_specs=[
            pl.BlockSpec((None, rows, d), lambda b, i: (b, i, 0)),
            pl.BlockSpec((None, rows, d_half), lambda b, i: (b, i, 0)),
            pl.BlockSpec((None, rows, d_half), lambda b, i: (b, i, 0)),
            pl.BlockSpec((None,) + mod.shape[1:], lambda b, i: (b, 0, 0)),
            const(d_half, d), const(d_half, d), const(1, d), const(1, d),
            const(d, d_ff), const(d, d_ff), const(d_ff, d), const(1, d), const(1, d),
        ],
        out_specs=pl.BlockSpec((None, rows, d), lambda b, i: (b, i, 0)),
        out_shape=jax.ShapeDtypeStruct((bsz, seq, d), x.dtype),
        compiler_params=pltpu.CompilerParams(
            dimension_semantics=("parallel", "parallel"),
            vmem_limit_bytes=56 * 1024 * 1024),
        name="out_ffn",
    )(x, a, g_out, mod, wo_a, wo_g, ln1_g, ln1_b, w_gate, w_up, w_down, ln2_g, ln2_b)


def kernel(x, c, w_ada, b_ada, w_in, lambda_q1, lambda_k1, lambda_q2, lambda_k2, subln_g,
           sgu_ln_g, sgu_ln_b, w_spatial, b_spatial, w_out, ln1_g, ln1_b, w_gate, w_up, w_down,
           ln2_g, ln2_b):
    depth = w_in.shape[0]
    d = x.shape[-1]
    d_attn = ATTN_HEADS * ATTN_V_DIM
    alpha = (2 * depth) ** 0.25
    for l in range(depth):
        lam_init = 0.8 - 0.6 * math.exp(-0.3 * l)
        lam_vecs = jnp.stack([lambda_q1[l], lambda_k1[l], lambda_q2[l], lambda_k2[l]])
        mod, lam = _adaln_mod(c, w_ada[l], b_ada[l], lam_vecs.astype(_F32), lam_init)

        w = w_in[l].astype(_BF16)
        wqt = w[:, :d_attn].T
        wk = w[:, d_attn:2 * d_attn]
        wvt = w[:, 2 * d_attn:3 * d_attn].T
        wz = w[:, 3 * d_attn:]
        qt, k, vt, g_out = _in_projection(
            x, mod, wqt, wk, wvt, wz,
            sgu_ln_g[l].reshape(1, -1), sgu_ln_b[l].reshape(1, -1),
            w_spatial[l], b_spatial[l].T)

        a = _diff_attention(lam, qt, k, vt, subln_g[l].reshape(-1, 1), 1.0 - lam_init)

        wo = w_out[l].astype(_BF16)
        x = _out_ffn(
            x, a, g_out, mod, wo[:d_attn], wo[d_attn:],
            ln1_g[l].reshape(1, d), ln1_b[l].reshape(1, d),
            w_gate[l].astype(_BF16), w_up[l].astype(_BF16), w_down[l].astype(_BF16),
            ln2_g[l].reshape(1, d), ln2_b[l].reshape(1, d), alpha)
    return x
```

```python
import functools
import math

import jax
import jax.numpy as jnp
from jax import lax
from jax.experimental import pallas as pl
from jax.experimental.pallas import tpu as pltpu

ATTN_HEADS = 4
ATTN_HEAD_DIM = 64
ATTN_V_DIM = 2 * ATTN_HEAD_DIM
SGU_GROUPS = 4
SGU_GROUP_DIM = 128
CHUNK = 128
LN_EPS = 1e-5
RMS_EPS = 1e-5

SUBLANES = 8
BF16_SUBLANES = 16
LANES = 128
V7X_VMEM_BYTES = 64 * 1024 * 1024

PROJ_ROWS = 1024
PROJ_SUBROWS = 256
ATTN_BLOCK = 512
ATTN_LEAD = 2
FFN_ROWS = 1024
FFN_SUBROWS = 256
FFN_COLS = 1024

_BF16 = jnp.bfloat16
_F32 = jnp.float32


def _dot(a, b):
    return jnp.dot(a, b, preferred_element_type=_F32)


def _dot_nt(a, b):
    return lax.dot_general(a, b, (((1,), (1,)), ((), ())), preferred_element_type=_F32)


def _nbytes(shape, dtype):
    return math.prod(shape) * jnp.dtype(dtype).itemsize


def _vmem_limit(pipelined, single, live):
    need = 2 * sum(pipelined) + sum(single) + live
    assert need <= V7X_VMEM_BYTES, need
    return need


def _layer_norm_rows(x, g, b):
    mu = jnp.mean(x, axis=-1, keepdims=True)
    xc = x - mu
    var = jnp.mean(xc * xc, axis=-1, keepdims=True)
    return xc * lax.rsqrt(var + LN_EPS) * g + b


def _mod_kernel(c_ref, w_ref, b_ref, lam_vec_ref, mod_ref, lam_ref, *, lam_init):
    c = c_ref[...]
    c_act = c * jax.nn.sigmoid(c)
    mod_ref[...] = jnp.dot(c_act, w_ref[...], preferred_element_type=_F32,
                           precision=lax.Precision.HIGHEST) + b_ref[...]
    lv = lam_vec_ref[...]
    s1 = jnp.sum(lv[0:1] * lv[1:2], axis=-1, keepdims=True)
    s2 = jnp.sum(lv[2:3] * lv[3:4], axis=-1, keepdims=True)
    lam = jnp.exp(s1) - jnp.exp(s2) + lam_init
    lam_ref[...] = jnp.broadcast_to(lam, lam_ref.shape)


def _adaln_mod(c, w_ada, b_ada, lam_vecs, lam_init):
    bsz, d = c.shape
    n = w_ada.shape[1]
    c_pad = jnp.zeros((SUBLANES, d), _F32).at[:bsz].set(c)
    mod, lam = pl.pallas_call(
        functools.partial(_mod_kernel, lam_init=lam_init),
        grid=(n // d,),
        in_specs=[
            pl.BlockSpec((SUBLANES, d), lambda j: (0, 0)),
            pl.BlockSpec((d, d), lambda j: (0, j)),
            pl.BlockSpec((1, d), lambda j: (0, j)),
            pl.BlockSpec(lam_vecs.shape, lambda j: (0, 0)),
        ],
        out_specs=[
            pl.BlockSpec((SUBLANES, d), lambda j: (0, j)),
            pl.BlockSpec((SUBLANES, LANES), lambda j: (0, 0)),
        ],
        out_shape=[
            jax.ShapeDtypeStruct((SUBLANES, n), _F32),
            jax.ShapeDtypeStruct((SUBLANES, LANES), _F32),
        ],
        name="adaln_mod",
    )(c_pad, w_ada, b_ada.reshape(1, n), lam_vecs)
    return mod[:bsz].reshape(bsz, n // d, d), lam[0, :1]


def _proj_kernel(x_ref, mod_ref, wqt_ref, wk_ref, wvt_ref, wz_ref, lng_ref, lnb_ref,
                 wsp_ref, bsp_ref, qt_ref, k_ref, vt_ref, g_ref, *, q_scale):
    rows = x_ref.shape[0]
    blk = vt_ref.shape[-1]
    mod = mod_ref[...]
    starts = list(range(0, rows, PROJ_SUBROWS))
    h = [(x_ref[r:r + PROJ_SUBROWS, :] * (1.0 + mod[1:2]) + mod[0:1]).astype(_BF16)
         for r in starts]
    z = [_dot(v, wz_ref[...]) for v in h]

    d_sgu = SGU_GROUPS * SGU_GROUP_DIM
    n_chunks = PROJ_SUBROWS // CHUNK
    t_idx = lax.broadcasted_iota(jnp.int32, (CHUNK, CHUNK), 0)
    s_idx = lax.broadcasted_iota(jnp.int32, (CHUNK, CHUNK), 1)
    causal = s_idx <= t_idx
    w_sp = [jnp.where(causal, wsp_ref[g], 0.0).astype(_BF16) for g in range(SGU_GROUPS)]
    for r, v, zr in zip(starts, h, z):
        k_ref[r:r + PROJ_SUBROWS, :] = _dot(v, wk_ref[...]).astype(_BF16)
        qt_ref[:, r:r + PROJ_SUBROWS] = (_dot_nt(wqt_ref[...], v) * q_scale).astype(_BF16)
        lane = r % blk
        vt_ref[r // blk, :, lane:lane + PROJ_SUBROWS] = _dot_nt(wvt_ref[...], v).astype(_BF16)
        zr = jax.nn.gelu(zr, approximate=True)
        for g in range(SGU_GROUPS):
            lo = g * SGU_GROUP_DIM
            hi = lo + SGU_GROUP_DIM
            u = zr[:, lo:hi]
            vs = zr[:, d_sgu + lo:d_sgu + hi]
            vn = _layer_norm_rows(vs, lng_ref[:, lo:hi], lnb_ref[:, lo:hi]).astype(_BF16)
            v_cat = jnp.concatenate(
                [vn[c * CHUNK:(c + 1) * CHUNK] for c in range(n_chunks)], axis=1)
            gate = _dot(w_sp[g], v_cat) + bsp_ref[:, g:g + 1]
            for c in range(n_chunks):
                g_ref[r + c * CHUNK:r + (c + 1) * CHUNK, lo:hi] = (
                    u[c * CHUNK:(c + 1) * CHUNK] * gate[:, c * CHUNK:(c + 1) * CHUNK]
                ).astype(_BF16)


def _in_projection(x, mod, wqt, wk, wvt, wz, ln_g, ln_b, w_sp, b_sp_t):
    bsz, seq, d = x.shape
    d_attn = wk.shape[1]
    d_z = wz.shape[1]
    rows = PROJ_ROWS
    blk = ATTN_BLOCK
    const = lambda *shape: pl.BlockSpec(shape, lambda b, i: (0,) * len(shape),
                                        pipeline_mode=pl.Buffered(1))
    vmem = _vmem_limit(
        pipelined=[_nbytes((rows, d), _F32), _nbytes(mod.shape[1:], _F32),
                   3 * _nbytes((rows, d_attn), _BF16), _nbytes((rows, d_z // 2), _BF16)],
        single=[3 * _nbytes((d, d_attn), _BF16), _nbytes((d, d_z), _BF16),
                _nbytes(w_sp.shape, _F32), _nbytes((CHUNK, LANES), _F32)],
        live=rows * (d_z * 4 + d * 2) + 2 * _nbytes((PROJ_SUBROWS, d_z), _F32))
    return pl.pallas_call(
        functools.partial(_proj_kernel, q_scale=ATTN_HEAD_DIM ** -0.5 * math.log2(math.e)),
        grid=(bsz, seq // rows),
        in_specs=[
            pl.BlockSpec((None, rows, d), lambda b, i: (b, i, 0)),
            pl.BlockSpec((None,) + mod.shape[1:], lambda b, i: (b, 0, 0)),
            const(d_attn, d), const(d, d_attn), const(d_attn, d), const(d, d_z),
            const(1, d_z // 2), const(1, d_z // 2),
            const(SGU_GROUPS, CHUNK, CHUNK), const(CHUNK, SGU_GROUPS),
        ],
        out_specs=[
            pl.BlockSpec((None, d_attn, rows), lambda b, i: (b, 0, i)),
            pl.BlockSpec((None, rows, d_attn), lambda b, i: (b, i, 0)),
            pl.BlockSpec((None, rows // blk, d_attn, blk), lambda b, i: (b, i, 0, 0)),
            pl.BlockSpec((None, rows, d_z // 2), lambda b, i: (b, i, 0)),
        ],
        out_shape=[
            jax.ShapeDtypeStruct((bsz, d_attn, seq), _BF16),
            jax.ShapeDtypeStruct((bsz, seq, d_attn), _BF16),
            jax.ShapeDtypeStruct((bsz, seq // blk, d_attn, blk), _BF16),
            jax.ShapeDtypeStruct((bsz, seq, d_z // 2), _BF16),
        ],
        compiler_params=pltpu.CompilerParams(
            dimension_semantics=("parallel", "parallel"),
            vmem_limit_bytes=vmem),
        name="in_projection",
    )(x, mod, wqt, wk, wvt, wz, ln_g, ln_b, w_sp, b_sp_t)


def _attn_kernel(lam_ref, qt_ref, k_ref, vt_ref, g_ref, o_ref,
                 qz_ref, s_ref, cmax_ref, m_ref, acc_ref, *, out_scale):
    blk = qt_ref.shape[-1]
    e_dim = ATTN_V_DIM
    i = pl.program_id(1)
    heads = range(ATTN_HEADS)
    cols = [slice(h * e_dim, (h + 1) * e_dim) for h in heads]

    for h in heads:
        qt = qt_ref[cols[h], :]
        row = lax.broadcasted_iota(jnp.int32, qt.shape, 0)
        zero = jnp.zeros_like(qt)
        qz_ref[h, :, :blk] = jnp.where(row < ATTN_HEAD_DIM, qt, zero)
        qz_ref[h, :, blk:] = jnp.where(row >= ATTN_HEAD_DIM, qt, zero)

    m_ref[...] = jnp.full(m_ref.shape, -jnp.inf, _F32)
    acc_ref[...] = jnp.zeros(acc_ref.shape, _F32)
    ones_rows = jnp.ones((acc_ref.shape[1] - e_dim, blk), _BF16)

    half = blk // 2
    late = [slice(half, blk), slice(blk + half, 2 * blk)]

    def causal(s, lane_offset):
        key = lax.broadcasted_iota(jnp.int32, s.shape, 0)
        qry = lax.broadcasted_iota(jnp.int32, s.shape, 1) % half + lane_offset
        return jnp.where(key <= qry, s, -jnp.inf)

    def produce(h, kb, masked):
        start = pl.multiple_of(kb * blk, blk)
        if not masked:
            s = _dot(k_ref[pl.ds(start, blk), cols[h]], qz_ref[h])
            s_ref[h] = s
            cmax_ref[h] = jnp.max(s, axis=0, keepdims=True)
            return
        early = _dot(k_ref[pl.ds(start, half), cols[h]], qz_ref[h])
        lane = lax.broadcasted_iota(jnp.int32, early.shape, 1) % blk
        early = jnp.where(lane >= half, early, causal(early, 0))
        s_ref[h, :half, :] = early
        cmax = jnp.max(early, axis=0, keepdims=True)
        q_late = jnp.concatenate([qz_ref[h, :, c] for c in late], axis=1)
        late_s = causal(_dot(k_ref[pl.ds(start + half, half), cols[h]], q_late), 0)
        for n, c in enumerate(late):
            piece = late_s[:, n * half:(n + 1) * half]
            s_ref[h, half:, c] = piece
            cmax_ref[h, :, c] = jnp.maximum(cmax[:, c], jnp.max(piece, axis=0, keepdims=True))
        for c in (slice(0, half), slice(blk, blk + half)):
            cmax_ref[h, :, c] = cmax[:, c]

    def consume(h, kb, masked):
        m_prev = m_ref[h]
        m_new = jnp.maximum(m_prev, cmax_ref[h])
        alpha = jnp.exp2(m_prev - m_new)
        m_ref[h] = m_new
        vt_ones = jnp.concatenate([vt_ref[kb, cols[h], :], ones_rows], axis=0)
        if not masked:
            e = jnp.exp2(s_ref[h] - m_new).astype(_BF16)
            acc_ref[h] = alpha * acc_ref[h] + _dot(vt_ones, e)
            return
        e = jnp.exp2(s_ref[h, :half, :] - m_new).astype(_BF16)
        pv = _dot(vt_ones[:, :half], e)
        for c in (slice(0, half), slice(blk, blk + half)):
            acc_ref[h, :, c] = alpha[:, c] * acc_ref[h, :, c] + pv[:, c]
        for c in late:
            e = jnp.exp2(s_ref[h, half:, c] - m_new[:, c]).astype(_BF16)
            acc_ref[h, :, c] = (alpha[:, c] * acc_ref[h, :, c] + pv[:, c]
                                + _dot(vt_ones[:, half:], e))

    def block_steps(kb, masked, next_masked):
        for h in heads:
            nxt = h + ATTN_LEAD
            if nxt < ATTN_HEADS:
                produce(nxt, kb, masked)
            elif next_masked is not None:
                produce(nxt - ATTN_HEADS, kb + 1, next_masked)
            consume(h, kb, masked)

    @pl.when(i == 0)
    def _():
        for h in range(ATTN_LEAD):
            produce(h, 0, True)

    @pl.when(i > 0)
    def _():
        for h in range(ATTN_LEAD):
            produce(h, 0, False)

    n_plain = jnp.maximum(i - 1, 0)

    def pair(u, carry):
        block_steps(2 * u, False, False)
        block_steps(2 * u + 1, False, False)
        return carry

    lax.fori_loop(0, lax.shift_right_logical(n_plain, 1), pair, 0)

    @pl.when((n_plain & 1) == 1)
    def _():
        block_steps(i - 2, False, False)

    @pl.when(i > 0)
    def _():
        block_steps(i - 1, False, True)

    block_steps(i, True, None)

    lam = lam_ref[0]
    for h in heads:
        acc = acc_ref[h]
        o = acc[:e_dim] / acc[e_dim:e_dim + 1]
        o = o[:, :blk] - lam * o[:, blk:]
        ms = jnp.mean(o * o, axis=0, keepdims=True)
        y = o * lax.rsqrt(ms + RMS_EPS) * g_ref[...] * out_scale
        o_ref[:, cols[h]] = y.T.astype(o_ref.dtype)


def _diff_attention(lam, qt, k, vt, subln_g_col, out_scale):
    bsz, d_attn, seq = qt.shape
    blk = ATTN_BLOCK
    e = ATTN_V_DIM
    scratch = [
        ((ATTN_HEADS, e, 2 * blk), _BF16),
        ((ATTN_HEADS, blk, 2 * blk), _F32),
        ((ATTN_HEADS, 1, 2 * blk), _F32),
        ((ATTN_HEADS, 1, 2 * blk), _F32),
        ((ATTN_HEADS, e + BF16_SUBLANES, 2 * blk), _F32),
    ]
    vmem = _vmem_limit(
        pipelined=[2 * _nbytes((blk, d_attn), _BF16), 2 * _nbytes((seq, d_attn), _BF16),
                   _nbytes((e, LANES), _F32)],
        single=[_nbytes(*s) for s in scratch],
        live=2 * (_nbytes((blk, 2 * blk), _F32) + _nbytes((blk, 2 * blk), _BF16)))
    return pl.pallas_call(
        functools.partial(_attn_kernel, out_scale=out_scale),
        grid=(bsz, seq // blk),
        in_specs=[
            pl.BlockSpec(memory_space=pltpu.SMEM),
            pl.BlockSpec((None, d_attn, blk), lambda b, i: (b, 0, i)),
            pl.BlockSpec((None, seq, d_attn), lambda b, i: (b, 0, 0)),
            pl.BlockSpec((None, seq // blk, d_attn, blk), lambda b, i: (b, 0, 0, 0)),
            pl.BlockSpec((e, 1), lambda b, i: (0, 0)),
        ],
        out_specs=pl.BlockSpec((None, blk, d_attn), lambda b, i: (b, i, 0)),
        out_shape=jax.ShapeDtypeStruct((bsz, seq, d_attn), _BF16),
        scratch_shapes=[pltpu.VMEM(*s) for s in scratch],
        compiler_params=pltpu.CompilerParams(
            dimension_semantics=("parallel", "arbitrary"),
            vmem_limit_bytes=vmem),
        name="diff_attention",
    )(lam, qt, k, vt, subln_g_col)


def _ffn_kernel(x_ref, a_ref, g_ref, mod_ref, wo_a_ref, wo_g_ref, ln1g_ref, ln1b_ref,
                wgate_ref, wup_ref, wdown_ref, ln2g_ref, ln2b_ref, o_ref, *, alpha):
    mod = mod_ref[...]
    rows = x_ref.shape[0]
    groups = [slice(r, r + FFN_SUBROWS) for r in range(0, rows, FFN_SUBROWS)]
    mixed = [_dot(a_ref[r, :], wo_a_ref[...]) + _dot(g_ref[r, :], wo_g_ref[...]) for r in groups]
    x1 = [_layer_norm_rows(alpha * x_ref[r, :] + (1.0 + mod[2:3]) * mx,
                           ln1g_ref[...], ln1b_ref[...]) for r, mx in zip(groups, mixed)]
    h = [(v * (1.0 + mod[4:5]) + mod[3:4]).astype(_BF16) for v in x1]
    d_ff = wgate_ref.shape[1]
    f = [None] * len(groups)
    for lo in range(0, d_ff, FFN_COLS):
        cols = slice(lo, min(lo + FFN_COLS, d_ff))
        gate = [_dot(v, wgate_ref[:, cols]) for v in h]
        up = [_dot(v, wup_ref[:, cols]) for v in h]
        act = [(gt * jax.nn.sigmoid(gt) * u).astype(_BF16) for gt, u in zip(gate, up)]
        part = [_dot(v, wdown_ref[cols, :]) for v in act]
        f = [p if acc is None else acc + p for acc, p in zip(f, part)]
    for r, v, fv in zip(groups, x1, f):
        o_ref[r, :] = _layer_norm_rows(alpha * v + (1.0 + mod[5:6]) * fv,
                                       ln2g_ref[...], ln2b_ref[...])


def _out_ffn(x, a, g_out, mod, wo_a, wo_g, ln1_g, ln1_b, w_gate, w_up, w_down, ln2_g, ln2_b,
             alpha):
    bsz, seq, d = x.shape
    rows = FFN_ROWS
    d_half = a.shape[-1]
    d_ff = w_gate.shape[1]
    const = lambda *shape: pl.BlockSpec(shape, lambda b, i: (0,) * len(shape),
                                        pipeline_mode=pl.Buffered(1))
    vmem = _vmem_limit(
        pipelined=[2 * _nbytes((rows, d), _F32), 2 * _nbytes((rows, d_half), _BF16),
                   _nbytes(mod.shape[1:], _F32)],
        single=[2 * _nbytes((d_half, d), _BF16), 3 * _nbytes((d, d_ff), _BF16),
                4 * _nbytes((SUBLANES, d), _F32)],
        live=rows * d * (4 + 4 + 2) + 2 * FFN_SUBROWS * FFN_COLS * (4 + 4 + 2))
    return pl.pallas_call(
        functools.partial(_ffn_kernel, alpha=alpha),
        grid=(bsz, seq // rows),
        in_specs=[
            pl.BlockSpec((None, rows, d), lambda b, i: (b, i, 0)),
            pl.BlockSpec((None, rows, d_half), lambda b, i: (b, i, 0)),
            pl.BlockSpec((None, rows, d_half), lambda b, i: (b, i, 0)),
            pl.BlockSpec((None,) + mod.shape[1:], lambda b, i: (b, 0, 0)),
            const(d_half, d), const(d_half, d), const(1, d), const(1, d),
            const(d, d_ff), const(d, d_ff), const(d_ff, d), const(1, d), const(1, d),
        ],
        out_specs=pl.BlockSpec((None, rows, d), lambda b, i: (b, i, 0)),
        out_shape=jax.ShapeDtypeStruct((bsz, seq, d), x.dtype),
        compiler_params=pltpu.CompilerParams(
            dimension_semantics=("parallel", "parallel"),
            vmem_limit_bytes=vmem),
        name="out_ffn",
    )(x, a, g_out, mod, wo_a, wo_g, ln1_g, ln1_b, w_gate, w_up, w_down, ln2_g, ln2_b)


def kernel(x, c, w_ada, b_ada, w_in, lambda_q1, lambda_k1, lambda_q2, lambda_k2, subln_g,
           sgu_ln_g, sgu_ln_b, w_spatial, b_spatial, w_out, ln1_g, ln1_b, w_gate, w_up, w_down,
           ln2_g, ln2_b):
    depth = w_in.shape[0]
    d = x.shape[-1]
    d_attn = ATTN_HEADS * ATTN_V_DIM
    alpha = (2 * depth) ** 0.25
    for l in range(depth):
        lam_init = 0.8 - 0.6 * math.exp(-0.3 * l)
        lam_vecs = jnp.stack([lambda_q1[l], lambda_k1[l], lambda_q2[l], lambda_k2[l]])
        mod, lam = _adaln_mod(c, w_ada[l], b_ada[l], lam_vecs.astype(_F32), lam_init)

        w = w_in[l].astype(_BF16)
        wqt = w[:, :d_attn].T
        wk = w[:, d_attn:2 * d_attn]
        wvt = w[:, 2 * d_attn:3 * d_attn].T
        wz = w[:, 3 * d_attn:]
        qt, k, vt, g_out = _in_projection(
            x, mod, wqt, wk, wvt, wz,
            sgu_ln_g[l].reshape(1, -1), sgu_ln_b[l].reshape(1, -1),
            w_spatial[l], b_spatial[l].T)

        a = _diff_attention(lam, qt, k, vt, subln_g[l].reshape(-1, 1), 1.0 - lam_init)

        wo = w_out[l].astype(_BF16)
        x = _out_ffn(
            x, a, g_out, mod, wo[:d_attn], wo[d_attn:],
            ln1_g[l].reshape(1, d), ln1_b[l].reshape(1, d),
            w_gate[l].astype(_BF16), w_up[l].astype(_BF16), w_down[l].astype(_BF16),
            ln2_g[l].reshape(1, d), ln2_b[l].reshape(1, d), alpha)
    return x
```

```python
import functools
import math

import jax
import jax.numpy as jnp
from jax import lax
from jax.experimental import pallas as pl
from jax.experimental.pallas import tpu as pltpu

ATTN_HEADS = 4
ATTN_HEAD_DIM = 64
ATTN_V_DIM = 2 * ATTN_HEAD_DIM
SGU_GROUPS = 4
SGU_GROUP_DIM = 128
CHUNK = 128
LN_EPS = 1e-5
RMS_EPS = 1e-5

SUBLANES = 8
BF16_SUBLANES = 16
LANES = 128
V7X_VMEM_BYTES = 64 * 1024 * 1024

PROJ_ROWS = 1024
PROJ_SUBROWS = 256
ATTN_BLOCK = 512
ATTN_LEAD = 2
FFN_ROWS = 1024
FFN_SUBROWS = 256
FFN_COLS = 1024

_BF16 = jnp.bfloat16
_F32 = jnp.float32


def _dot(a, b):
    return jnp.dot(a, b, preferred_element_type=_F32)


def _dot_nt(a, b):
    return lax.dot_general(a, b, (((1,), (1,)), ((), ())), preferred_element_type=_F32)


def _nbytes(shape, dtype):
    return math.prod(shape) * jnp.dtype(dtype).itemsize


def _vmem_limit(pipelined, single, live):
    need = 2 * sum(pipelined) + sum(single) + live
    assert need <= V7X_VMEM_BYTES, need
    return need


def _layer_norm_rows(x, g, b):
    mu = jnp.mean(x, axis=-1, keepdims=True)
    xc = x - mu
    var = jnp.mean(xc * xc, axis=-1, keepdims=True)
    return xc * lax.rsqrt(var + LN_EPS) * g + b


def _mod_kernel(c_ref, w_ref, b_ref, lam_vec_ref, mod_ref, lam_ref, *, lam_init):
    c = c_ref[...]
    c_act = c * jax.nn.sigmoid(c)
    mod_ref[...] = _dot(c_act.astype(_BF16), w_ref[...].astype(_BF16)) + b_ref[...]
    lv = lam_vec_ref[...]
    s1 = jnp.sum(lv[0:1] * lv[1:2], axis=-1, keepdims=True)
    s2 = jnp.sum(lv[2:3] * lv[3:4], axis=-1, keepdims=True)
    lam = jnp.exp(s1) - jnp.exp(s2) + lam_init
    lam_ref[...] = jnp.broadcast_to(lam, lam_ref.shape)


def _adaln_mod(c, w_ada, b_ada, lam_vecs, lam_init):
    bsz, d = c.shape
    n = w_ada.shape[1]
    c_pad = jnp.zeros((SUBLANES, d), _F32).at[:bsz].set(c)
    mod, lam = pl.pallas_call(
        functools.partial(_mod_kernel, lam_init=lam_init),
        grid=(n // d,),
        in_specs=[
            pl.BlockSpec((SUBLANES, d), lambda j: (0, 0)),
            pl.BlockSpec((d, d), lambda j: (0, j)),
            pl.BlockSpec((1, d), lambda j: (0, j)),
            pl.BlockSpec(lam_vecs.shape, lambda j: (0, 0)),
        ],
        out_specs=[
            pl.BlockSpec((SUBLANES, d), lambda j: (0, j)),
            pl.BlockSpec((SUBLANES, LANES), lambda j: (0, 0)),
        ],
        out_shape=[
            jax.ShapeDtypeStruct((SUBLANES, n), _F32),
            jax.ShapeDtypeStruct((SUBLANES, LANES), _F32),
        ],
        name="adaln_mod",
    )(c_pad, w_ada, b_ada.reshape(1, n), lam_vecs)
    return mod[:bsz].reshape(bsz, n // d, d), lam[0, :1]


def _proj_kernel(x_ref, mod_ref, wqt_ref, wk_ref, wvt_ref, wz_ref, lng_ref, lnb_ref,
                 wsp_ref, bsp_ref, qt_ref, k_ref, vt_ref, g_ref, *, q_scale):
    rows = x_ref.shape[0]
    blk = vt_ref.shape[-1]
    mod = mod_ref[...]
    starts = list(range(0, rows, PROJ_SUBROWS))
    h = [(x_ref[r:r + PROJ_SUBROWS, :] * (1.0 + mod[1:2]) + mod[0:1]).astype(_BF16)
         for r in starts]
    z = [_dot(v, wz_ref[...]) for v in h]

    d_sgu = SGU_GROUPS * SGU_GROUP_DIM
    n_chunks = PROJ_SUBROWS // CHUNK
    t_idx = lax.broadcasted_iota(jnp.int32, (CHUNK, CHUNK), 0)
    s_idx = lax.broadcasted_iota(jnp.int32, (CHUNK, CHUNK), 1)
    causal = s_idx <= t_idx
    w_sp = [jnp.where(causal, wsp_ref[g], 0.0).astype(_BF16) for g in range(SGU_GROUPS)]
    for r, v, zr in zip(starts, h, z):
        k_ref[r:r + PROJ_SUBROWS, :] = _dot(v, wk_ref[...]).astype(_BF16)
        qt_ref[:, r:r + PROJ_SUBROWS] = (_dot_nt(wqt_ref[...], v) * q_scale).astype(_BF16)
        lane = r % blk
        vt_ref[r // blk, :, lane:lane + PROJ_SUBROWS] = _dot_nt(wvt_ref[...], v).astype(_BF16)
        zr = jax.nn.gelu(zr, approximate=True)
        for g in range(SGU_GROUPS):
            lo = g * SGU_GROUP_DIM
            hi = lo + SGU_GROUP_DIM
            u = zr[:, lo:hi]
            vs = zr[:, d_sgu + lo:d_sgu + hi]
            vn = _layer_norm_rows(vs, lng_ref[:, lo:hi], lnb_ref[:, lo:hi]).astype(_BF16)
            v_cat = jnp.concatenate(
                [vn[c * CHUNK:(c + 1) * CHUNK] for c in range(n_chunks)], axis=1)
            gate = _dot(w_sp[g], v_cat) + bsp_ref[:, g:g + 1]
            for c in range(n_chunks):
                g_ref[r + c * CHUNK:r + (c + 1) * CHUNK, lo:hi] = (
                    u[c * CHUNK:(c + 1) * CHUNK] * gate[:, c * CHUNK:(c + 1) * CHUNK]
                ).astype(_BF16)


def _in_projection(x, mod, wqt, wk, wvt, wz, ln_g, ln_b, w_sp, b_sp_t):
    bsz, seq, d = x.shape
    d_attn = wk.shape[1]
    d_z = wz.shape[1]
    rows = PROJ_ROWS
    blk = ATTN_BLOCK
    const = lambda *shape: pl.BlockSpec(shape, lambda b, i: (0,) * len(shape),
                                        pipeline_mode=pl.Buffered(1))
    vmem = _vmem_limit(
        pipelined=[_nbytes((rows, d), _F32), _nbytes(mod.shape[1:], _F32),
                   3 * _nbytes((rows, d_attn), _BF16), _nbytes((rows, d_z // 2), _BF16)],
        single=[3 * _nbytes((d, d_attn), _BF16), _nbytes((d, d_z), _BF16),
                _nbytes(w_sp.shape, _F32), _nbytes((CHUNK, LANES), _F32)],
        live=rows * (d_z * 4 + d * 2) + 2 * _nbytes((PROJ_SUBROWS, d_z), _F32))
    return pl.pallas_call(
        functools.partial(_proj_kernel, q_scale=ATTN_HEAD_DIM ** -0.5 * math.log2(math.e)),
        grid=(bsz, seq // rows),
        in_specs=[
            pl.BlockSpec((None, rows, d), lambda b, i: (b, i, 0)),
            pl.BlockSpec((None,) + mod.shape[1:], lambda b, i: (b, 0, 0)),
            const(d_attn, d), const(d, d_attn), const(d_attn, d), const(d, d_z),
            const(1, d_z // 2), const(1, d_z // 2),
            const(SGU_GROUPS, CHUNK, CHUNK), const(CHUNK, SGU_GROUPS),
        ],
        out_specs=[
            pl.BlockSpec((None, d_attn, rows), lambda b, i: (b, 0, i)),
            pl.BlockSpec((None, rows, d_attn), lambda b, i: (b, i, 0)),
            pl.BlockSpec((None, rows // blk, d_attn, blk), lambda b, i: (b, i, 0, 0)),
            pl.BlockSpec((None, rows, d_z // 2), lambda b, i: (b, i, 0)),
        ],
        out_shape=[
            jax.ShapeDtypeStruct((bsz, d_attn, seq), _BF16),
            jax.ShapeDtypeStruct((bsz, seq, d_attn), _BF16),
            jax.ShapeDtypeStruct((bsz, seq // blk, d_attn, blk), _BF16),
            jax.ShapeDtypeStruct((bsz, seq, d_z // 2), _BF16),
        ],
        compiler_params=pltpu.CompilerParams(
            dimension_semantics=("parallel", "parallel"),
            vmem_limit_bytes=vmem),
        name="in_projection",
    )(x, mod, wqt, wk, wvt, wz, ln_g, ln_b, w_sp, b_sp_t)


def _attn_kernel(lam_ref, qt_ref, k_ref, vt_ref, g_ref, o_ref,
                 qz_ref, s_ref, cmax_ref, m_ref, acc_ref, *, out_scale):
    blk = qt_ref.shape[-1]
    e_dim = ATTN_V_DIM
    i = pl.program_id(1)
    heads = range(ATTN_HEADS)
    cols = [slice(h * e_dim, (h + 1) * e_dim) for h in heads]

    for h in heads:
        qt = qt_ref[cols[h], :]
        row = lax.broadcasted_iota(jnp.int32, qt.shape, 0)
        zero = jnp.zeros_like(qt)
        qz_ref[h, :, :blk] = jnp.where(row < ATTN_HEAD_DIM, qt, zero)
        qz_ref[h, :, blk:] = jnp.where(row >= ATTN_HEAD_DIM, qt, zero)

    m_ref[...] = jnp.full(m_ref.shape, -jnp.inf, _F32)
    acc_ref[...] = jnp.zeros(acc_ref.shape, _F32)
    ones_rows = jnp.ones((acc_ref.shape[1] - e_dim, blk), _BF16)

    half = blk // 2
    late = [slice(half, blk), slice(blk + half, 2 * blk)]

    def causal(s, lane_offset):
        key = lax.broadcasted_iota(jnp.int32, s.shape, 0)
        qry = lax.broadcasted_iota(jnp.int32, s.shape, 1) % half + lane_offset
        return jnp.where(key <= qry, s, -jnp.inf)

    def produce(h, kb, masked):
        start = pl.multiple_of(kb * blk, blk)
        if not masked:
            s = _dot(k_ref[pl.ds(start, blk), cols[h]], qz_ref[h])
            s_ref[h] = s
            cmax_ref[h] = jnp.max(s, axis=0, keepdims=True)
            return
        early = _dot(k_ref[pl.ds(start, half), cols[h]], qz_ref[h])
        lane = lax.broadcasted_iota(jnp.int32, early.shape, 1) % blk
        early = jnp.where(lane >= half, early, causal(early, 0))
        s_ref[h, :half, :] = early
        cmax = jnp.max(early, axis=0, keepdims=True)
        q_late = jnp.concatenate([qz_ref[h, :, c] for c in late], axis=1)
        late_s = causal(_dot(k_ref[pl.ds(start + half, half), cols[h]], q_late), 0)
        for n, c in enumerate(late):
            piece = late_s[:, n * half:(n + 1) * half]
            s_ref[h, half:, c] = piece
            cmax_ref[h, :, c] = jnp.maximum(cmax[:, c], jnp.max(piece, axis=0, keepdims=True))
        for c in (slice(0, half), slice(blk, blk + half)):
            cmax_ref[h, :, c] = cmax[:, c]

    def consume(h, kb, masked):
        m_prev = m_ref[h]
        m_new = jnp.maximum(m_prev, cmax_ref[h])
        alpha = jnp.exp2(m_prev - m_new)
        m_ref[h] = m_new
        vt_ones = jnp.concatenate([vt_ref[kb, cols[h], :], ones_rows], axis=0)
        if not masked:
            e = jnp.exp2(s_ref[h] - m_new).astype(_BF16)
            acc_ref[h] = alpha * acc_ref[h] + _dot(vt_ones, e)
            return
        e = jnp.exp2(s_ref[h, :half, :] - m_new).astype(_BF16)
        pv = _dot(vt_ones[:, :half], e)
        for c in (slice(0, half), slice(blk, blk + half)):
            acc_ref[h, :, c] = alpha[:, c] * acc_ref[h, :, c] + pv[:, c]
        for c in late:
            e = jnp.exp2(s_ref[h, half:, c] - m_new[:, c]).astype(_BF16)
            acc_ref[h, :, c] = (alpha[:, c] * acc_ref[h, :, c] + pv[:, c]
                                + _dot(vt_ones[:, half:], e))

    def block_steps(kb, masked, next_masked):
        for h in heads:
            nxt = h + ATTN_LEAD
            if nxt < ATTN_HEADS:
                produce(nxt, kb, masked)
            elif next_masked is not None:
                produce(nxt - ATTN_HEADS, kb + 1, next_masked)
            consume(h, kb, masked)

    @pl.when(i > 0)
    def _():
        for h in range(ATTN_LEAD):
            produce(h, 0, False)

    n_plain = jnp.maximum(i - 1, 0)

    def pair(u, carry):
        block_steps(2 * u, False, False)
        block_steps(2 * u + 1, False, False)
        return carry

    lax.fori_loop(0, lax.shift_right_logical(n_plain, 1), pair, 0)

    odd_plain = (n_plain & 1) == 1

    @pl.when(i == 0)
    def _():
        for h in range(ATTN_LEAD):
            produce(h, 0, True)
        block_steps(i, True, None)

    @pl.when(jnp.logical_and(i > 0, jnp.logical_not(odd_plain)))
    def _():
        block_steps(i - 1, False, True)
        block_steps(i, True, None)

    @pl.when(odd_plain)
    def _():
        block_steps(i - 2, False, False)
        block_steps(i - 1, False, True)
        block_steps(i, True, None)

    lam = lam_ref[0]
    for h in heads:
        acc = acc_ref[h]
        o = acc[:e_dim] / acc[e_dim:e_dim + 1]
        o = o[:, :blk] - lam * o[:, blk:]
        ms = jnp.mean(o * o, axis=0, keepdims=True)
        y = o * lax.rsqrt(ms + RMS_EPS) * g_ref[...] * out_scale
        o_ref[:, cols[h]] = y.T.astype(o_ref.dtype)


def _diff_attention(lam, qt, k, vt, subln_g_col, out_scale):
    bsz, d_attn, seq = qt.shape
    blk = ATTN_BLOCK
    e = ATTN_V_DIM
    scratch = [
        ((ATTN_HEADS, e, 2 * blk), _BF16),
        ((ATTN_HEADS, blk, 2 * blk), _F32),
        ((ATTN_HEADS, 1, 2 * blk), _F32),
        ((ATTN_HEADS, 1, 2 * blk), _F32),
        ((ATTN_HEADS, e + BF16_SUBLANES, 2 * blk), _F32),
    ]
    vmem = _vmem_limit(
        pipelined=[2 * _nbytes((blk, d_attn), _BF16), 2 * _nbytes((seq, d_attn), _BF16),
                   _nbytes((e, LANES), _F32)],
        single=[_nbytes(*s) for s in scratch],
        live=2 * (_nbytes((blk, 2 * blk), _F32) + _nbytes((blk, 2 * blk), _BF16)))
    return pl.pallas_call(
        functools.partial(_attn_kernel, out_scale=out_scale),
        grid=(bsz, seq // blk),
        in_specs=[
            pl.BlockSpec(memory_space=pltpu.SMEM),
            pl.BlockSpec((None, d_attn, blk), lambda b, i: (b, 0, i)),
            pl.BlockSpec((None, seq, d_attn), lambda b, i: (b, 0, 0)),
            pl.BlockSpec((None, seq // blk, d_attn, blk), lambda b, i: (b, 0, 0, 0)),
            pl.BlockSpec((e, 1), lambda b, i: (0, 0)),
        ],
        out_specs=pl.BlockSpec((None, blk, d_attn), lambda b, i: (b, i, 0)),
        out_shape=jax.ShapeDtypeStruct((bsz, seq, d_attn), _BF16),
        scratch_shapes=[pltpu.VMEM(*s) for s in scratch],
        compiler_params=pltpu.CompilerParams(
            dimension_semantics=("parallel", "arbitrary"),
            vmem_limit_bytes=vmem),
        name="diff_attention",
    )(lam, qt, k, vt, subln_g_col)


def _ffn_kernel(x_ref, a_ref, g_ref, mod_ref, wo_a_ref, wo_g_ref, ln1g_ref, ln1b_ref,
                wgate_ref, wup_ref, wdown_ref, ln2g_ref, ln2b_ref, o_ref, *, alpha):
    mod = mod_ref[...]
    rows = x_ref.shape[0]
    groups = [slice(r, r + FFN_SUBROWS) for r in range(0, rows, FFN_SUBROWS)]
    mixed = [_dot(a_ref[r, :], wo_a_ref[...]) + _dot(g_ref[r, :], wo_g_ref[...]) for r in groups]
    x1 = [_layer_norm_rows(alpha * x_ref[r, :] + (1.0 + mod[2:3]) * mx,
                           ln1g_ref[...], ln1b_ref[...]) for r, mx in zip(groups, mixed)]
    h = [(v * (1.0 + mod[4:5]) + mod[3:4]).astype(_BF16) for v in x1]
    d_ff = wgate_ref.shape[1]
    f = [None] * len(groups)
    for lo in range(0, d_ff, FFN_COLS):
        cols = slice(lo, min(lo + FFN_COLS, d_ff))
        gate = [_dot(v, wgate_ref[:, cols]) for v in h]
        up = [_dot(v, wup_ref[:, cols]) for v in h]
        act = [(gt * jax.nn.sigmoid(gt) * u).astype(_BF16) for gt, u in zip(gate, up)]
        part = [_dot(v, wdown_ref[cols, :]) for v in act]
        f = [p if acc is None else acc + p for acc, p in zip(f, part)]
    for r, v, fv in zip(groups, x1, f):
        o_ref[r, :] = _layer_norm_rows(alpha * v + (1.0 + mod[5:6]) * fv,
                                       ln2g_ref[...], ln2b_ref[...])


def _out_ffn(x, a, g_out, mod, wo_a, wo_g, ln1_g, ln1_b, w_gate, w_up, w_down, ln2_g, ln2_b,
             alpha):
    bsz, seq, d = x.shape
    rows = FFN_ROWS
    d_half = a.shape[-1]
    d_ff = w_gate.shape[1]
    const = lambda *shape: pl.BlockSpec(shape, lambda b, i: (0,) * len(shape),
                                        pipeline_mode=pl.Buffered(1))
    vmem = _vmem_limit(
        pipelined=[2 * _nbytes((rows, d), _F32), 2 * _nbytes((rows, d_half), _BF16),
                   _nbytes(mod.shape[1:], _F32)],
        single=[2 * _nbytes((d_half, d), _BF16), 3 * _nbytes((d, d_ff), _BF16),
                4 * _nbytes((SUBLANES, d), _F32)],
        live=rows * d * (4 + 4 + 2) + 2 * FFN_SUBROWS * FFN_COLS * (4 + 4 + 2))
    return pl.pallas_call(
        functools.partial(_ffn_kernel, alpha=alpha),
        grid=(bsz, seq // rows),
        in_specs=[
            pl.BlockSpec((None, rows, d), lambda b, i: (b, i, 0)),
            pl.BlockSpec((None, rows, d_half), lambda b, i: (b, i, 0)),
            pl.BlockSpec((None, rows, d_half), lambda b, i: (b, i, 0)),
            pl.BlockSpec((None,) + mod.shape[1:], lambda b, i: (b, 0, 0)),
            const(d_half, d), const(d_half, d), const(1, d), const(1, d),
            const(d, d_ff), const(d, d_ff), const(d_ff, d), const(1, d), const(1, d),
        ],
        out_specs=pl.BlockSpec((None, rows, d), lambda b, i: (b, i, 0)),
        out_shape=jax.ShapeDtypeStruct((bsz, seq, d), x.dtype),
        compiler_params=pltpu.CompilerParams(
            dimension_semantics=("parallel", "parallel"),
            vmem_limit_bytes=vmem),
        name="out_ffn",
    )(x, a, g_out, mod, wo_a, wo_g, ln1_g, ln1_b, w_gate, w_up, w_down, ln2_g, ln2_b)


def kernel(x, c, w_ada, b_ada, w_in, lambda_q1, lambda_k1, lambda_q2, lambda_k2, subln_g,
           sgu_ln_g, sgu_ln_b, w_spatial, b_spatial, w_out, ln1_g, ln1_b, w_gate, w_up, w_down,
           ln2_g, ln2_b):
    depth = w_in.shape[0]
    d = x.shape[-1]
    d_attn = ATTN_HEADS * ATTN_V_DIM
    alpha = (2 * depth) ** 0.25
    for l in range(depth):
        lam_init = 0.8 - 0.6 * math.exp(-0.3 * l)
        lam_vecs = jnp.stack([lambda_q1[l], lambda_k1[l], lambda_q2[l], lambda_k2[l]])
        mod, lam = _adaln_mod(c, w_ada[l], b_ada[l], lam_vecs.astype(_F32), lam_init)

        w = w_in[l].astype(_BF16)
        wqt = w[:, :d_attn].T
        wk = w[:, d_attn:2 * d_attn]
        wvt = w[:, 2 * d_attn:3 * d_attn].T
        wz = w[:, 3 * d_attn:]
        qt, k, vt, g_out = _in_projection(
            x, mod, wqt, wk, wvt, wz,
            sgu_ln_g[l].reshape(1, -1), sgu_ln_b[l].reshape(1, -1),
            w_spatial[l], b_spatial[l].T)

        a = _diff_attention(lam, qt, k, vt, subln_g[l].reshape(-1, 1), 1.0 - lam_init)

        wo = w_out[l].astype(_BF16)
        x = _out_ffn(
            x, a, g_out, mod, wo[:d_attn], wo[d_attn:],
            ln1_g[l].reshape(1, d), ln1_b[l].reshape(1, d),
            w_gate[l].astype(_BF16), w_up[l].astype(_BF16), w_down[l].astype(_BF16),
            ln2_g[l].reshape(1, d), ln2_b[l].reshape(1, d), alpha)
    return x
```

```python
import functools
import math

import jax
import jax.numpy as jnp
from jax import lax
from jax.experimental import pallas as pl
from jax.experimental.pallas import tpu as pltpu

ATTN_HEADS = 4
ATTN_HEAD_DIM = 64
ATTN_V_DIM = 2 * ATTN_HEAD_DIM
SGU_GROUPS = 4
SGU_GROUP_DIM = 128
CHUNK = 128
LN_EPS = 1e-5
RMS_EPS = 1e-5

SUBLANES = 8
BF16_SUBLANES = 16
LANES = 128
V7X_VMEM_BYTES = 64 * 1024 * 1024

PROJ_ROWS = 1024
PROJ_SUBROWS = 256
ATTN_BLOCK = 512
ATTN_LEAD = 2
FFN_ROWS = 1024
FFN_SUBROWS = 256
FFN_COLS = 1024

_BF16 = jnp.bfloat16
_F32 = jnp.float32


def _dot(a, b):
    return jnp.dot(a, b, preferred_element_type=_F32)


def _dot_nt(a, b):
    return lax.dot_general(a, b, (((1,), (1,)), ((), ())), preferred_element_type=_F32)


def _nbytes(shape, dtype):
    return math.prod(shape) * jnp.dtype(dtype).itemsize


def _vmem_limit(pipelined, single, live):
    need = 2 * sum(pipelined) + sum(single) + live
    assert need <= V7X_VMEM_BYTES, need
    return need


def _layer_norm_rows(x, g, b):
    mu = jnp.mean(x, axis=-1, keepdims=True)
    xc = x - mu
    var = jnp.mean(xc * xc, axis=-1, keepdims=True)
    return xc * lax.rsqrt(var + LN_EPS) * g + b


def _mod_kernel(c_ref, w_ref, b_ref, lam_vec_ref, mod_ref, lam_ref, *, lam_init):
    c = c_ref[...]
    c_act = c * jax.nn.sigmoid(c)
    mod_ref[...] = _dot(c_act.astype(_BF16), w_ref[...].astype(_BF16)) + b_ref[...]
    lv = lam_vec_ref[...]
    s1 = jnp.sum(lv[0:1] * lv[1:2], axis=-1, keepdims=True)
    s2 = jnp.sum(lv[2:3] * lv[3:4], axis=-1, keepdims=True)
    lam = jnp.exp(s1) - jnp.exp(s2) + lam_init
    lam_ref[...] = jnp.broadcast_to(lam, lam_ref.shape)


def _adaln_mod(c, w_ada, b_ada, lam_vecs, lam_init):
    bsz, d = c.shape
    n = w_ada.shape[1]
    c_pad = jnp.zeros((SUBLANES, d), _F32).at[:bsz].set(c)
    mod, lam = pl.pallas_call(
        functools.partial(_mod_kernel, lam_init=lam_init),
        grid=(n // d,),
        in_specs=[
            pl.BlockSpec((SUBLANES, d), lambda j: (0, 0)),
            pl.BlockSpec((d, d), lambda j: (0, j)),
            pl.BlockSpec((1, d), lambda j: (0, j)),
            pl.BlockSpec(lam_vecs.shape, lambda j: (0, 0)),
        ],
        out_specs=[
            pl.BlockSpec((SUBLANES, d), lambda j: (0, j)),
            pl.BlockSpec((SUBLANES, LANES), lambda j: (0, 0)),
        ],
        out_shape=[
            jax.ShapeDtypeStruct((SUBLANES, n), _F32),
            jax.ShapeDtypeStruct((SUBLANES, LANES), _F32),
        ],
        name="adaln_mod",
    )(c_pad, w_ada, b_ada.reshape(1, n), lam_vecs)
    return mod[:bsz].reshape(bsz, n // d, d), lam[0, :1]


def _proj_kernel(x_ref, mod_ref, wqt_ref, wk_ref, wvt_ref, wz_ref, lng_ref, lnb_ref,
                 wsp_ref, bsp_ref, qt_ref, k_ref, vt_ref, g_ref, *, q_scale):
    rows = x_ref.shape[0]
    blk = vt_ref.shape[-1]
    mod = mod_ref[...]
    starts = list(range(0, rows, PROJ_SUBROWS))
    h = [(x_ref[r:r + PROJ_SUBROWS, :] * (1.0 + mod[1:2]) + mod[0:1]).astype(_BF16)
         for r in starts]
    z = [_dot(v, wz_ref[...]) for v in h]

    d_sgu = SGU_GROUPS * SGU_GROUP_DIM
    n_chunks = PROJ_SUBROWS // CHUNK
    t_idx = lax.broadcasted_iota(jnp.int32, (CHUNK, CHUNK), 0)
    s_idx = lax.broadcasted_iota(jnp.int32, (CHUNK, CHUNK), 1)
    causal = s_idx <= t_idx
    w_sp = [jnp.where(causal, wsp_ref[g], 0.0).astype(_BF16) for g in range(SGU_GROUPS)]
    for r, v, zr in zip(starts, h, z):
        k = _dot(v, wk_ref[...]).astype(_BF16)
        for head in range(ATTN_HEADS):
            k_ref[head, r:r + PROJ_SUBROWS, :] = k[:, head * ATTN_V_DIM:(head + 1) * ATTN_V_DIM]
        qt_ref[:, r:r + PROJ_SUBROWS] = (_dot_nt(wqt_ref[...], v) * q_scale).astype(_BF16)
        lane = r % blk
        vt_ref[r // blk, :, lane:lane + PROJ_SUBROWS] = _dot_nt(wvt_ref[...], v).astype(_BF16)
        zr = jax.nn.gelu(zr, approximate=True)
        for g in range(SGU_GROUPS):
            lo = g * SGU_GROUP_DIM
            hi = lo + SGU_GROUP_DIM
            u = zr[:, lo:hi]
            vs = zr[:, d_sgu + lo:d_sgu + hi]
            vn = _layer_norm_rows(vs, lng_ref[:, lo:hi], lnb_ref[:, lo:hi]).astype(_BF16)
            v_cat = jnp.concatenate(
                [vn[c * CHUNK:(c + 1) * CHUNK] for c in range(n_chunks)], axis=1)
            gate = _dot(w_sp[g], v_cat) + bsp_ref[:, g:g + 1]
            for c in range(n_chunks):
                g_ref[r + c * CHUNK:r + (c + 1) * CHUNK, lo:hi] = (
                    u[c * CHUNK:(c + 1) * CHUNK] * gate[:, c * CHUNK:(c + 1) * CHUNK]
                ).astype(_BF16)


def _in_projection(x, mod, wqt, wk, wvt, wz, ln_g, ln_b, w_sp, b_sp_t):
    bsz, seq, d = x.shape
    d_attn = wk.shape[1]
    d_z = wz.shape[1]
    rows = PROJ_ROWS
    blk = ATTN_BLOCK
    const = lambda *shape: pl.BlockSpec(shape, lambda b, i: (0,) * len(shape),
                                        pipeline_mode=pl.Buffered(1))
    vmem = _vmem_limit(
        pipelined=[_nbytes((rows, d), _F32), _nbytes(mod.shape[1:], _F32),
                   3 * _nbytes((rows, d_attn), _BF16), _nbytes((rows, d_z // 2), _BF16)],
        single=[3 * _nbytes((d, d_attn), _BF16), _nbytes((d, d_z), _BF16),
                _nbytes(w_sp.shape, _F32), _nbytes((CHUNK, LANES), _F32)],
        live=(_nbytes((rows, d_z), _F32) + _nbytes((rows, d), _BF16)
              + 2 * _nbytes((PROJ_SUBROWS, d_z), _F32)))
    return pl.pallas_call(
        functools.partial(_proj_kernel, q_scale=ATTN_HEAD_DIM ** -0.5 * math.log2(math.e)),
        grid=(bsz, seq // rows),
        in_specs=[
            pl.BlockSpec((None, rows, d), lambda b, i: (b, i, 0)),
            pl.BlockSpec((None,) + mod.shape[1:], lambda b, i: (b, 0, 0)),
            const(d_attn, d), const(d, d_attn), const(d_attn, d), const(d, d_z),
            const(1, d_z // 2), const(1, d_z // 2),
            const(SGU_GROUPS, CHUNK, CHUNK), const(CHUNK, SGU_GROUPS),
        ],
        out_specs=[
            pl.BlockSpec((None, d_attn, rows), lambda b, i: (b, 0, i)),
            pl.BlockSpec((None, ATTN_HEADS, rows, ATTN_V_DIM), lambda b, i: (b, 0, i, 0)),
            pl.BlockSpec((None, rows // blk, d_attn, blk), lambda b, i: (b, i, 0, 0)),
            pl.BlockSpec((None, rows, d_z // 2), lambda b, i: (b, i, 0)),
        ],
        out_shape=[
            jax.ShapeDtypeStruct((bsz, d_attn, seq), _BF16),
            jax.ShapeDtypeStruct((bsz, ATTN_HEADS, seq, ATTN_V_DIM), _BF16),
            jax.ShapeDtypeStruct((bsz, seq // blk, d_attn, blk), _BF16),
            jax.ShapeDtypeStruct((bsz, seq, d_z // 2), _BF16),
        ],
        compiler_params=pltpu.CompilerParams(
            dimension_semantics=("parallel", "parallel"),
            vmem_limit_bytes=vmem),
        name="in_projection",
    )(x, mod, wqt, wk, wvt, wz, ln_g, ln_b, w_sp, b_sp_t)


def _attn_kernel(lam_ref, qt_ref, k_ref, vt_ref, g_ref, o_ref,
                 qz_ref, s_ref, cmax_ref, m_ref, acc_ref, *, out_scale):
    blk = qt_ref.shape[-1]
    e_dim = ATTN_V_DIM
    i = pl.program_id(1)
    heads = range(ATTN_HEADS)
    cols = [slice(h * e_dim, (h + 1) * e_dim) for h in heads]

    for h in heads:
        qt = qt_ref[cols[h], :]
        row = lax.broadcasted_iota(jnp.int32, qt.shape, 0)
        zero = jnp.zeros_like(qt)
        qz_ref[h, :, :blk] = jnp.where(row < ATTN_HEAD_DIM, qt, zero)
        qz_ref[h, :, blk:] = jnp.where(row >= ATTN_HEAD_DIM, qt, zero)

    m_ref[...] = jnp.full(m_ref.shape, -jnp.inf, _F32)
    acc_ref[...] = jnp.zeros(acc_ref.shape, _F32)
    ones_rows = jnp.ones((acc_ref.shape[1] - e_dim, blk), _BF16)

    half = blk // 2
    late = [slice(half, blk), slice(blk + half, 2 * blk)]

    def causal(s, lane_offset):
        key = lax.broadcasted_iota(jnp.int32, s.shape, 0)
        qry = lax.broadcasted_iota(jnp.int32, s.shape, 1) % half + lane_offset
        return jnp.where(key <= qry, s, -jnp.inf)

    def produce(h, kb, masked):
        start = pl.multiple_of(kb * blk, blk)
        if not masked:
            s = _dot(k_ref[h, pl.ds(start, blk), :], qz_ref[h])
            s_ref[h] = s
            cmax_ref[h] = jnp.max(s, axis=0, keepdims=True)
            return
        early = _dot(k_ref[h, pl.ds(start, half), :], qz_ref[h])
        lane = lax.broadcasted_iota(jnp.int32, early.shape, 1) % blk
        early = jnp.where(lane >= half, early, causal(early, 0))
        s_ref[h, :half, :] = early
        cmax = jnp.max(early, axis=0, keepdims=True)
        q_late = jnp.concatenate([qz_ref[h, :, c] for c in late], axis=1)
        late_s = causal(_dot(k_ref[h, pl.ds(start + half, half), :], q_late), 0)
        for n, c in enumerate(late):
            piece = late_s[:, n * half:(n + 1) * half]
            s_ref[h, half:, c] = piece
            cmax_ref[h, :, c] = jnp.maximum(cmax[:, c], jnp.max(piece, axis=0, keepdims=True))
        for c in (slice(0, half), slice(blk, blk + half)):
            cmax_ref[h, :, c] = cmax[:, c]

    def consume(h, kb, masked):
        m_prev = m_ref[h]
        m_new = jnp.maximum(m_prev, cmax_ref[h])
        alpha = jnp.exp2(m_prev - m_new)
        m_ref[h] = m_new
        vt_ones = jnp.concatenate([vt_ref[kb, cols[h], :], ones_rows], axis=0)
        if not masked:
            e = jnp.exp2(s_ref[h] - m_new).astype(_BF16)
            acc_ref[h] = alpha * acc_ref[h] + _dot(vt_ones, e)
            return
        e = jnp.exp2(s_ref[h, :half, :] - m_new).astype(_BF16)
        pv = _dot(vt_ones[:, :half], e)
        for c in (slice(0, half), slice(blk, blk + half)):
            acc_ref[h, :, c] = alpha[:, c] * acc_ref[h, :, c] + pv[:, c]
        for c in late:
            e = jnp.exp2(s_ref[h, half:, c] - m_new[:, c]).astype(_BF16)
            acc_ref[h, :, c] = (alpha[:, c] * acc_ref[h, :, c] + pv[:, c]
                                + _dot(vt_ones[:, half:], e))

    def block_steps(kb, masked, next_masked):
        for h in heads:
            nxt = h + ATTN_LEAD
            if nxt < ATTN_HEADS:
                produce(nxt, kb, masked)
            elif next_masked is not None:
                produce(nxt - ATTN_HEADS, kb + 1, next_masked)
            consume(h, kb, masked)

    @pl.when(i > 0)
    def _():
        for h in range(ATTN_LEAD):
            produce(h, 0, False)

    n_plain = jnp.maximum(i - 1, 0)

    def pair(u, carry):
        block_steps(2 * u, False, False)
        block_steps(2 * u + 1, False, False)
        return carry

    lax.fori_loop(0, lax.shift_right_logical(n_plain, 1), pair, 0)

    odd_plain = (n_plain & 1) == 1

    @pl.when(i == 0)
    def _():
        for h in range(ATTN_LEAD):
            produce(h, 0, True)
        block_steps(i, True, None)

    @pl.when(jnp.logical_and(i > 0, jnp.logical_not(odd_plain)))
    def _():
        block_steps(i - 1, False, True)
        block_steps(i, True, None)

    @pl.when(odd_plain)
    def _():
        block_steps(i - 2, False, False)
        block_steps(i - 1, False, True)
        block_steps(i, True, None)

    lam = lam_ref[0]
    for h in heads:
        acc = acc_ref[h]
        o = acc[:e_dim] / acc[e_dim:e_dim + 1]
        o = o[:, :blk] - lam * o[:, blk:]
        ms = jnp.mean(o * o, axis=0, keepdims=True)
        y = o * lax.rsqrt(ms + RMS_EPS) * g_ref[...] * out_scale
        o_ref[:, cols[h]] = y.T.astype(o_ref.dtype)


def _diff_attention(lam, qt, k, vt, subln_g_col, out_scale):
    bsz, d_attn, seq = qt.shape
    blk = ATTN_BLOCK
    e = ATTN_V_DIM
    scratch = [
        ((ATTN_HEADS, e, 2 * blk), _BF16),
        ((ATTN_HEADS, blk, 2 * blk), _F32),
        ((ATTN_HEADS, 1, 2 * blk), _F32),
        ((ATTN_HEADS, 1, 2 * blk), _F32),
        ((ATTN_HEADS, e + BF16_SUBLANES, 2 * blk), _F32),
    ]
    vmem = _vmem_limit(
        pipelined=[2 * _nbytes((blk, d_attn), _BF16), 2 * _nbytes((seq, d_attn), _BF16),
                   _nbytes((e, LANES), _F32)],
        single=[_nbytes(*s) for s in scratch],
        live=2 * (_nbytes((blk, 2 * blk), _F32) + _nbytes((blk, 2 * blk), _BF16)))
    return pl.pallas_call(
        functools.partial(_attn_kernel, out_scale=out_scale),
        grid=(bsz, seq // blk),
        in_specs=[
            pl.BlockSpec(memory_space=pltpu.SMEM),
            pl.BlockSpec((None, d_attn, blk), lambda b, i: (b, 0, i)),
            pl.BlockSpec((None, ATTN_HEADS, seq, e), lambda b, i: (b, 0, 0, 0)),
            pl.BlockSpec((None, seq // blk, d_attn, blk), lambda b, i: (b, 0, 0, 0)),
            pl.BlockSpec((e, 1), lambda b, i: (0, 0)),
        ],
        out_specs=pl.BlockSpec((None, blk, d_attn), lambda b, i: (b, i, 0)),
        out_shape=jax.ShapeDtypeStruct((bsz, seq, d_attn), _BF16),
        scratch_shapes=[pltpu.VMEM(*s) for s in scratch],
        compiler_params=pltpu.CompilerParams(
            dimension_semantics=("parallel", "arbitrary"),
            vmem_limit_bytes=vmem),
        name="diff_attention",
    )(lam, qt, k, vt, subln_g_col)


def _ffn_kernel(x_ref, a_ref, g_ref, mod_ref, wo_a_ref, wo_g_ref, ln1g_ref, ln1b_ref,
                wgate_ref, wup_ref, wdown_ref, ln2g_ref, ln2b_ref, o_ref, *, alpha):
    mod = mod_ref[...]
    rows = x_ref.shape[0]
    groups = [slice(r, r + FFN_SUBROWS) for r in range(0, rows, FFN_SUBROWS)]
    mixed = [_dot(a_ref[r, :], wo_a_ref[...]) + _dot(g_ref[r, :], wo_g_ref[...]) for r in groups]
    x1 = [_layer_norm_rows(alpha * x_ref[r, :] + (1.0 + mod[2:3]) * mx,
                           ln1g_ref[...], ln1b_ref[...]) for r, mx in zip(groups, mixed)]
    h = [(v * (1.0 + mod[4:5]) + mod[3:4]).astype(_BF16) for v in x1]
    d_ff = wgate_ref.shape[1]
    f = [None] * len(groups)
    for lo in range(0, d_ff, FFN_COLS):
        cols = slice(lo, min(lo + FFN_COLS, d_ff))
        gate = [_dot(v, wgate_ref[:, cols]) for v in h]
        up = [_dot(v, wup_ref[:, cols]) for v in h]
        act = [(gt * jax.nn.sigmoid(gt) * u).astype(_BF16) for gt, u in zip(gate, up)]
        part = [_dot(v, wdown_ref[cols, :]) for v in act]
        f = [p if acc is None else acc + p for acc, p in zip(f, part)]
    for r, v, fv in zip(groups, x1, f):
        o_ref[r, :] = _layer_norm_rows(alpha * v + (1.0 + mod[5:6]) * fv,
                                       ln2g_ref[...], ln2b_ref[...])


def _out_ffn(x, a, g_out, mod, wo_a, wo_g, ln1_g, ln1_b, w_gate, w_up, w_down, ln2_g, ln2_b,
             alpha):
    bsz, seq, d = x.shape
    rows = FFN_ROWS
    d_half = a.shape[-1]
    d_ff = w_gate.shape[1]
    const = lambda *shape: pl.BlockSpec(shape, lambda b, i: (0,) * len(shape),
                                        pipeline_mode=pl.Buffered(1))
    vmem = _vmem_limit(
        pipelined=[2 * _nbytes((rows, d), _F32), 2 * _nbytes((rows, d_half), _BF16),
                   _nbytes(mod.shape[1:], _F32)],
        single=[2 * _nbytes((d_half, d), _BF16), 3 * _nbytes((d, d_ff), _BF16),
                4 * _nbytes((SUBLANES, d), _F32)],
        live=(2 * _nbytes((rows, d), _F32) + _nbytes((rows, d), _BF16)
              + 2 * (2 * _nbytes((FFN_SUBROWS, FFN_COLS), _F32)
                     + _nbytes((FFN_SUBROWS, FFN_COLS), _BF16))))
    return pl.pallas_call(
        functools.partial(_ffn_kernel, alpha=alpha),
        grid=(bsz, seq // rows),
        in_specs=[
            pl.BlockSpec((None, rows, d), lambda b, i: (b, i, 0)),
            pl.BlockSpec((None, rows, d_half), lambda b, i: (b, i, 0)),
            pl.BlockSpec((None, rows, d_half), lambda b, i: (b, i, 0)),
            pl.BlockSpec((None,) + mod.shape[1:], lambda b, i: (b, 0, 0)),
            const(d_half, d), const(d_half, d), const(1, d), const(1, d),
            const(d, d_ff), const(d, d_ff), const(d_ff, d), const(1, d), const(1, d),
        ],
        out_specs=pl.BlockSpec((None, rows, d), lambda b, i: (b, i, 0)),
        out_shape=jax.ShapeDtypeStruct((bsz, seq, d), x.dtype),
        compiler_params=pltpu.CompilerParams(
            dimension_semantics=("parallel", "parallel"),
            vmem_limit_bytes=vmem),
        name="out_ffn",
    )(x, a, g_out, mod, wo_a, wo_g, ln1_g, ln1_b, w_gate, w_up, w_down, ln2_g, ln2_b)


def kernel(x, c, w_ada, b_ada, w_in, lambda_q1, lambda_k1, lambda_q2, lambda_k2, subln_g,
           sgu_ln_g, sgu_ln_b, w_spatial, b_spatial, w_out, ln1_g, ln1_b, w_gate, w_up, w_down,
           ln2_g, ln2_b):
    depth = w_in.shape[0]
    d = x.shape[-1]
    d_attn = ATTN_HEADS * ATTN_V_DIM
    alpha = (2 * depth) ** 0.25
    for l in range(depth):
        lam_init = 0.8 - 0.6 * math.exp(-0.3 * l)
        lam_vecs = jnp.stack([lambda_q1[l], lambda_k1[l], lambda_q2[l], lambda_k2[l]])
        mod, lam = _adaln_mod(c, w_ada[l], b_ada[l], lam_vecs.astype(_F32), lam_init)

        w = w_in[l].astype(_BF16)
        wqt = w[:, :d_attn].T
        wk = w[:, d_attn:2 * d_attn]
        wvt = w[:, 2 * d_attn:3 * d_attn].T
        wz = w[:, 3 * d_attn:]
        qt, k, vt, g_out = _in_projection(
            x, mod, wqt, wk, wvt, wz,
            sgu_ln_g[l].reshape(1, -1), sgu_ln_b[l].reshape(1, -1),
            w_spatial[l], b_spatial[l].T)

        a = _diff_attention(lam, qt, k, vt, subln_g[l].reshape(-1, 1), 1.0 - lam_init)

        wo = w_out[l].astype(_BF16)
        x = _out_ffn(
            x, a, g_out, mod, wo[:d_attn], wo[d_attn:],
            ln1_g[l].reshape(1, d), ln1_b[l].reshape(1, d),
            w_gate[l].astype(_BF16), w_up[l].astype(_BF16), w_down[l].astype(_BF16),
            ln2_g[l].reshape(1, d), ln2_b[l].reshape(1, d), alpha)
    return x
```

```python
import functools
import math

import jax
import jax.numpy as jnp
from jax import lax
from jax.experimental import pallas as pl
from jax.experimental.pallas import tpu as pltpu

ATTN_HEADS = 4
ATTN_HEAD_DIM = 64
ATTN_V_DIM = 2 * ATTN_HEAD_DIM
SGU_GROUPS = 4
SGU_GROUP_DIM = 128
CHUNK = 128
LN_EPS = 1e-5
RMS_EPS = 1e-5

SUBLANES = 8
BF16_SUBLANES = 16
LANES = 128
V7X_VMEM_BYTES = 64 * 1024 * 1024

PROJ_ROWS = 1024
PROJ_SUBROWS = 512
ATTN_BLOCK = 512
ATTN_LEAD = 2
FFN_ROWS = 1024
FFN_SUBROWS = 256
FFN_COLS = 1536

_BF16 = jnp.bfloat16
_F32 = jnp.float32


def _dot(a, b):
    return jnp.dot(a, b, preferred_element_type=_F32)


def _dot_nt(a, b):
    return lax.dot_general(a, b, (((1,), (1,)), ((), ())), preferred_element_type=_F32)


def _nbytes(shape, dtype):
    return math.prod(shape) * jnp.dtype(dtype).itemsize


def _vmem_limit(pipelined, single, live):
    need = 2 * sum(pipelined) + sum(single) + live
    assert need <= V7X_VMEM_BYTES, need
    return need


def _layer_norm_rows(x, g, b):
    mu = jnp.mean(x, axis=-1, keepdims=True)
    xc = x - mu
    var = jnp.mean(xc * xc, axis=-1, keepdims=True)
    return xc * lax.rsqrt(var + LN_EPS) * g + b


def _mod_kernel(c_ref, w_ref, b_ref, lam_vec_ref, mod_ref, lam_ref, *, lam_init):
    c = c_ref[...]
    c_act = c * jax.nn.sigmoid(c)
    mod_ref[...] = _dot(c_act.astype(_BF16), w_ref[...].astype(_BF16)) + b_ref[...]
    lv = lam_vec_ref[...]
    s1 = jnp.sum(lv[0:1] * lv[1:2], axis=-1, keepdims=True)
    s2 = jnp.sum(lv[2:3] * lv[3:4], axis=-1, keepdims=True)
    lam = jnp.exp(s1) - jnp.exp(s2) + lam_init
    lam_ref[...] = jnp.broadcast_to(lam, lam_ref.shape)


def _adaln_mod(c, w_ada, b_ada, lam_vecs, lam_init):
    bsz, d = c.shape
    n = w_ada.shape[1]
    c_pad = jnp.zeros((SUBLANES, d), _F32).at[:bsz].set(c)
    mod, lam = pl.pallas_call(
        functools.partial(_mod_kernel, lam_init=lam_init),
        grid=(n // d,),
        in_specs=[
            pl.BlockSpec((SUBLANES, d), lambda j: (0, 0)),
            pl.BlockSpec((d, d), lambda j: (0, j)),
            pl.BlockSpec((1, d), lambda j: (0, j)),
            pl.BlockSpec(lam_vecs.shape, lambda j: (0, 0)),
        ],
        out_specs=[
            pl.BlockSpec((SUBLANES, d), lambda j: (0, j)),
            pl.BlockSpec((SUBLANES, LANES), lambda j: (0, 0)),
        ],
        out_shape=[
            jax.ShapeDtypeStruct((SUBLANES, n), _F32),
            jax.ShapeDtypeStruct((SUBLANES, LANES), _F32),
        ],
        name="adaln_mod",
    )(c_pad, w_ada, b_ada.reshape(1, n), lam_vecs)
    return mod[:bsz].reshape(bsz, n // d, d), lam[0, :1]


def _proj_kernel(x_ref, mod_ref, wqt_ref, wk_ref, wvt_ref, wz_ref, lng_ref, lnb_ref,
                 wsp_ref, bsp_ref, qt_ref, k_ref, vt_ref, g_ref, *, q_scale):
    rows = x_ref.shape[0]
    blk = vt_ref.shape[-1]
    mod = mod_ref[...]
    starts = list(range(0, rows, PROJ_SUBROWS))
    h = [(x_ref[r:r + PROJ_SUBROWS, :] * (1.0 + mod[1:2]) + mod[0:1]).astype(_BF16)
         for r in starts]
    z = [_dot(v, wz_ref[...]) for v in h]

    d_sgu = SGU_GROUPS * SGU_GROUP_DIM
    n_chunks = PROJ_SUBROWS // CHUNK
    t_idx = lax.broadcasted_iota(jnp.int32, (CHUNK, CHUNK), 0)
    s_idx = lax.broadcasted_iota(jnp.int32, (CHUNK, CHUNK), 1)
    causal = s_idx <= t_idx
    w_sp = [jnp.where(causal, wsp_ref[g], 0.0).astype(_BF16) for g in range(SGU_GROUPS)]
    for r, v, zr in zip(starts, h, z):
        k = _dot(v, wk_ref[...]).astype(_BF16)
        for head in range(ATTN_HEADS):
            k_ref[head, r:r + PROJ_SUBROWS, :] = k[:, head * ATTN_V_DIM:(head + 1) * ATTN_V_DIM]
        qt_ref[:, r:r + PROJ_SUBROWS] = (_dot_nt(wqt_ref[...], v) * q_scale).astype(_BF16)
        lane = r % blk
        vt_ref[r // blk, :, lane:lane + PROJ_SUBROWS] = _dot_nt(wvt_ref[...], v).astype(_BF16)
        zr = jax.nn.gelu(zr, approximate=True)
        for g in range(SGU_GROUPS):
            lo = g * SGU_GROUP_DIM
            hi = lo + SGU_GROUP_DIM
            u = zr[:, lo:hi]
            vs = zr[:, d_sgu + lo:d_sgu + hi]
            vn = _layer_norm_rows(vs, lng_ref[:, lo:hi], lnb_ref[:, lo:hi]).astype(_BF16)
            v_cat = jnp.concatenate(
                [vn[c * CHUNK:(c + 1) * CHUNK] for c in range(n_chunks)], axis=1)
            gate = _dot(w_sp[g], v_cat) + bsp_ref[:, g:g + 1]
            for c in range(n_chunks):
                g_ref[r + c * CHUNK:r + (c + 1) * CHUNK, lo:hi] = (
                    u[c * CHUNK:(c + 1) * CHUNK] * gate[:, c * CHUNK:(c + 1) * CHUNK]
                ).astype(_BF16)


def _in_projection(x, mod, wqt, wk, wvt, wz, ln_g, ln_b, w_sp, b_sp_t):
    bsz, seq, d = x.shape
    d_attn = wk.shape[1]
    d_z = wz.shape[1]
    rows = PROJ_ROWS
    blk = ATTN_BLOCK
    const = lambda *shape: pl.BlockSpec(shape, lambda b, i: (0,) * len(shape),
                                        pipeline_mode=pl.Buffered(1))
    vmem = _vmem_limit(
        pipelined=[_nbytes((rows, d), _F32), _nbytes(mod.shape[1:], _F32),
                   3 * _nbytes((rows, d_attn), _BF16), _nbytes((rows, d_z // 2), _BF16)],
        single=[3 * _nbytes((d, d_attn), _BF16), _nbytes((d, d_z), _BF16),
                _nbytes(w_sp.shape, _F32), _nbytes((CHUNK, LANES), _F32)],
        live=(_nbytes((rows, d_z), _F32) + _nbytes((rows, d), _BF16)
              + 2 * _nbytes((PROJ_SUBROWS, d_z), _F32)))
    return pl.pallas_call(
        functools.partial(_proj_kernel, q_scale=ATTN_HEAD_DIM ** -0.5 * math.log2(math.e)),
        grid=(bsz, seq // rows),
        in_specs=[
            pl.BlockSpec((None, rows, d), lambda b, i: (b, i, 0)),
            pl.BlockSpec((None,) + mod.shape[1:], lambda b, i: (b, 0, 0)),
            const(d_attn, d), const(d, d_attn), const(d_attn, d), const(d, d_z),
            const(1, d_z // 2), const(1, d_z // 2),
            const(SGU_GROUPS, CHUNK, CHUNK), const(CHUNK, SGU_GROUPS),
        ],
        out_specs=[
            pl.BlockSpec((None, d_attn, rows), lambda b, i: (b, 0, i)),
            pl.BlockSpec((None, ATTN_HEADS, rows, ATTN_V_DIM), lambda b, i: (b, 0, i, 0)),
            pl.BlockSpec((None, rows // blk, d_attn, blk), lambda b, i: (b, i, 0, 0)),
            pl.BlockSpec((None, rows, d_z // 2), lambda b, i: (b, i, 0)),
        ],
        out_shape=[
            jax.ShapeDtypeStruct((bsz, d_attn, seq), _BF16),
            jax.ShapeDtypeStruct((bsz, ATTN_HEADS, seq, ATTN_V_DIM), _BF16),
            jax.ShapeDtypeStruct((bsz, seq // blk, d_attn, blk), _BF16),
            jax.ShapeDtypeStruct((bsz, seq, d_z // 2), _BF16),
        ],
        compiler_params=pltpu.CompilerParams(
            dimension_semantics=("parallel", "parallel"),
            vmem_limit_bytes=vmem),
        name="in_projection",
    )(x, mod, wqt, wk, wvt, wz, ln_g, ln_b, w_sp, b_sp_t)


def _attn_kernel(lam_ref, qt_ref, k_ref, vt_ref, g_ref, o_ref,
                 qz_ref, s_ref, cmax_ref, m_ref, acc_ref, *, out_scale):
    blk = qt_ref.shape[-1]
    e_dim = ATTN_V_DIM
    i = pl.program_id(1)
    heads = range(ATTN_HEADS)
    cols = [slice(h * e_dim, (h + 1) * e_dim) for h in heads]

    for h in heads:
        qt = qt_ref[cols[h], :]
        row = lax.broadcasted_iota(jnp.int32, qt.shape, 0)
        zero = jnp.zeros_like(qt)
        qz_ref[h, :, :blk] = jnp.where(row < ATTN_HEAD_DIM, qt, zero)
        qz_ref[h, :, blk:] = jnp.where(row >= ATTN_HEAD_DIM, qt, zero)

    m_ref[...] = jnp.full(m_ref.shape, -jnp.inf, _F32)
    acc_ref[...] = jnp.zeros(acc_ref.shape, _F32)
    ones_rows = jnp.ones((acc_ref.shape[1] - e_dim, blk), _BF16)

    half = blk // 2
    late = [slice(half, blk), slice(blk + half, 2 * blk)]

    def causal(s, lane_offset):
        key = lax.broadcasted_iota(jnp.int32, s.shape, 0)
        qry = lax.broadcasted_iota(jnp.int32, s.shape, 1) % half + lane_offset
        return jnp.where(key <= qry, s, -jnp.inf)

    def produce(h, kb, masked):
        start = pl.multiple_of(kb * blk, blk)
        if not masked:
            s = _dot(k_ref[h, pl.ds(start, blk), :], qz_ref[h])
            s_ref[h] = s
            cmax_ref[h] = jnp.max(s, axis=0, keepdims=True)
            return
        early = _dot(k_ref[h, pl.ds(start, half), :], qz_ref[h])
        lane = lax.broadcasted_iota(jnp.int32, early.shape, 1) % blk
        early = jnp.where(lane >= half, early, causal(early, 0))
        s_ref[h, :half, :] = early
        cmax = jnp.max(early, axis=0, keepdims=True)
        q_late = jnp.concatenate([qz_ref[h, :, c] for c in late], axis=1)
        late_s = causal(_dot(k_ref[h, pl.ds(start + half, half), :], q_late), 0)
        for n, c in enumerate(late):
            piece = late_s[:, n * half:(n + 1) * half]
            s_ref[h, half:, c] = piece
            cmax_ref[h, :, c] = jnp.maximum(cmax[:, c], jnp.max(piece, axis=0, keepdims=True))
        for c in (slice(0, half), slice(blk, blk + half)):
            cmax_ref[h, :, c] = cmax[:, c]

    def consume(h, kb, masked):
        m_prev = m_ref[h]
        m_new = jnp.maximum(m_prev, cmax_ref[h])
        alpha = jnp.exp2(m_prev - m_new)
        m_ref[h] = m_new
        vt_ones = jnp.concatenate([vt_ref[kb, cols[h], :], ones_rows], axis=0)
        if not masked:
            e = jnp.exp2(s_ref[h] - m_new).astype(_BF16)
            acc_ref[h] = alpha * acc_ref[h] + _dot(vt_ones, e)
            return
        e = jnp.exp2(s_ref[h, :half, :] - m_new).astype(_BF16)
        pv = _dot(vt_ones[:, :half], e)
        for c in (slice(0, half), slice(blk, blk + half)):
            acc_ref[h, :, c] = alpha[:, c] * acc_ref[h, :, c] + pv[:, c]
        for c in late:
            e = jnp.exp2(s_ref[h, half:, c] - m_new[:, c]).astype(_BF16)
            acc_ref[h, :, c] = (alpha[:, c] * acc_ref[h, :, c] + pv[:, c]
                                + _dot(vt_ones[:, half:], e))

    def block_steps(kb, masked, next_masked):
        for h in heads:
            nxt = h + ATTN_LEAD
            if nxt < ATTN_HEADS:
                produce(nxt, kb, masked)
            elif next_masked is not None:
                produce(nxt - ATTN_HEADS, kb + 1, next_masked)
            consume(h, kb, masked)

    @pl.when(i > 0)
    def _():
        for h in range(ATTN_LEAD):
            produce(h, 0, False)

    n_plain = jnp.maximum(i - 1, 0)

    def pair(u, carry):
        block_steps(2 * u, False, False)
        block_steps(2 * u + 1, False, False)
        return carry

    lax.fori_loop(0, lax.shift_right_logical(n_plain, 1), pair, 0)

    odd_plain = (n_plain & 1) == 1

    @pl.when(i == 0)
    def _():
        for h in range(ATTN_LEAD):
            produce(h, 0, True)
        block_steps(i, True, None)

    @pl.when(jnp.logical_and(i > 0, jnp.logical_not(odd_plain)))
    def _():
        block_steps(i - 1, False, True)
        block_steps(i, True, None)

    @pl.when(odd_plain)
    def _():
        block_steps(i - 2, False, False)
        block_steps(i - 1, False, True)
        block_steps(i, True, None)

    lam = lam_ref[0]
    for h in heads:
        acc = acc_ref[h]
        o = acc[:e_dim] / acc[e_dim:e_dim + 1]
        o = o[:, :blk] - lam * o[:, blk:]
        ms = jnp.mean(o * o, axis=0, keepdims=True)
        y = o * lax.rsqrt(ms + RMS_EPS) * g_ref[...] * out_scale
        o_ref[:, cols[h]] = y.T.astype(o_ref.dtype)


def _diff_attention(lam, qt, k, vt, subln_g_col, out_scale):
    bsz, d_attn, seq = qt.shape
    blk = ATTN_BLOCK
    e = ATTN_V_DIM
    scratch = [
        ((ATTN_HEADS, e, 2 * blk), _BF16),
        ((ATTN_HEADS, blk, 2 * blk), _F32),
        ((ATTN_HEADS, 1, 2 * blk), _F32),
        ((ATTN_HEADS, 1, 2 * blk), _F32),
        ((ATTN_HEADS, e + BF16_SUBLANES, 2 * blk), _F32),
    ]
    vmem = _vmem_limit(
        pipelined=[2 * _nbytes((blk, d_attn), _BF16), 2 * _nbytes((seq, d_attn), _BF16),
                   _nbytes((e, LANES), _F32)],
        single=[_nbytes(*s) for s in scratch],
        live=2 * (_nbytes((blk, 2 * blk), _F32) + _nbytes((blk, 2 * blk), _BF16)))
    return pl.pallas_call(
        functools.partial(_attn_kernel, out_scale=out_scale),
        grid=(bsz, seq // blk),
        in_specs=[
            pl.BlockSpec(memory_space=pltpu.SMEM),
            pl.BlockSpec((None, d_attn, blk), lambda b, i: (b, 0, i)),
            pl.BlockSpec((None, ATTN_HEADS, seq, e), lambda b, i: (b, 0, 0, 0)),
            pl.BlockSpec((None, seq // blk, d_attn, blk), lambda b, i: (b, 0, 0, 0)),
            pl.BlockSpec((e, 1), lambda b, i: (0, 0)),
        ],
        out_specs=pl.BlockSpec((None, blk, d_attn), lambda b, i: (b, i, 0)),
        out_shape=jax.ShapeDtypeStruct((bsz, seq, d_attn), _BF16),
        scratch_shapes=[pltpu.VMEM(*s) for s in scratch],
        compiler_params=pltpu.CompilerParams(
            dimension_semantics=("parallel", "arbitrary"),
            vmem_limit_bytes=vmem),
        name="diff_attention",
    )(lam, qt, k, vt, subln_g_col)


def _ffn_kernel(x_ref, a_ref, g_ref, mod_ref, wo_a_ref, wo_g_ref, ln1g_ref, ln1b_ref,
                wgate_ref, wup_ref, wdown_ref, ln2g_ref, ln2b_ref, o_ref, *, alpha):
    mod = mod_ref[...]
    rows = x_ref.shape[0]
    groups = [slice(r, r + FFN_SUBROWS) for r in range(0, rows, FFN_SUBROWS)]
    mixed = [_dot(a_ref[r, :], wo_a_ref[...]) + _dot(g_ref[r, :], wo_g_ref[...]) for r in groups]
    x1 = [_layer_norm_rows(alpha * x_ref[r, :] + (1.0 + mod[2:3]) * mx,
                           ln1g_ref[...], ln1b_ref[...]) for r, mx in zip(groups, mixed)]
    h = [(v * (1.0 + mod[4:5]) + mod[3:4]).astype(_BF16) for v in x1]
    d_ff = wgate_ref.shape[1]
    f = [None] * len(groups)
    for lo in range(0, d_ff, FFN_COLS):
        cols = slice(lo, min(lo + FFN_COLS, d_ff))
        gate = [_dot(v, wgate_ref[:, cols]) for v in h]
        up = [_dot(v, wup_ref[:, cols]) for v in h]
        act = [(gt * jax.nn.sigmoid(gt) * u).astype(_BF16) for gt, u in zip(gate, up)]
        part = [_dot(v, wdown_ref[cols, :]) for v in act]
        f = [p if acc is None else acc + p for acc, p in zip(f, part)]
    for r, v, fv in zip(groups, x1, f):
        o_ref[r, :] = _layer_norm_rows(alpha * v + (1.0 + mod[5:6]) * fv,
                                       ln2g_ref[...], ln2b_ref[...])


def _out_ffn(x, a, g_out, mod, wo_a, wo_g, ln1_g, ln1_b, w_gate, w_up, w_down, ln2_g, ln2_b,
             alpha):
    bsz, seq, d = x.shape
    rows = FFN_ROWS
    d_half = a.shape[-1]
    d_ff = w_gate.shape[1]
    const = lambda *shape: pl.BlockSpec(shape, lambda b, i: (0,) * len(shape),
                                        pipeline_mode=pl.Buffered(1))
    vmem = _vmem_limit(
        pipelined=[2 * _nbytes((rows, d), _F32), 2 * _nbytes((rows, d_half), _BF16),
                   _nbytes(mod.shape[1:], _F32)],
        single=[2 * _nbytes((d_half, d), _BF16), 3 * _nbytes((d, d_ff), _BF16),
                4 * _nbytes((SUBLANES, d), _F32)],
        live=(2 * _nbytes((rows, d), _F32) + _nbytes((rows, d), _BF16)
              + 2 * (2 * _nbytes((FFN_SUBROWS, FFN_COLS), _F32)
                     + _nbytes((FFN_SUBROWS, FFN_COLS), _BF16))))
    return pl.pallas_call(
        functools.partial(_ffn_kernel, alpha=alpha),
        grid=(bsz, seq // rows),
        in_specs=[
            pl.BlockSpec((None, rows, d), lambda b, i: (b, i, 0)),
            pl.BlockSpec((None, rows, d_half), lambda b, i: (b, i, 0)),
            pl.BlockSpec((None, rows, d_half), lambda b, i: (b, i, 0)),
            pl.BlockSpec((None,) + mod.shape[1:], lambda b, i: (b, 0, 0)),
            const(d_half, d), const(d_half, d), const(1, d), const(1, d),
            const(d, d_ff), const(d, d_ff), const(d_ff, d), const(1, d), const(1, d),
        ],
        out_specs=pl.BlockSpec((None, rows, d), lambda b, i: (b, i, 0)),
        out_shape=jax.ShapeDtypeStruct((bsz, seq, d), x.dtype),
        compiler_params=pltpu.CompilerParams(
            dimension_semantics=("parallel", "parallel"),
            vmem_limit_bytes=vmem),
        name="out_ffn",
    )(x, a, g_out, mod, wo_a, wo_g, ln1_g, ln1_b, w_gate, w_up, w_down, ln2_g, ln2_b)


def kernel(x, c, w_ada, b_ada, w_in, lambda_q1, lambda_k1, lambda_q2, lambda_k2, subln_g,
           sgu_ln_g, sgu_ln_b, w_spatial, b_spatial, w_out, ln1_g, ln1_b, w_gate, w_up, w_down,
           ln2_g, ln2_b):
    depth = w_in.shape[0]
    d = x.shape[-1]
    d_attn = ATTN_HEADS * ATTN_V_DIM
    alpha = (2 * depth) ** 0.25
    for l in range(depth):
        lam_init = 0.8 - 0.6 * math.exp(-0.3 * l)
        lam_vecs = jnp.stack([lambda_q1[l], lambda_k1[l], lambda_q2[l], lambda_k2[l]])
        mod, lam = _adaln_mod(c, w_ada[l], b_ada[l], lam_vecs.astype(_F32), lam_init)

        w = w_in[l].astype(_BF16)
        wqt = w[:, :d_attn].T
        wk = w[:, d_attn:2 * d_attn]
        wvt = w[:, 2 * d_attn:3 * d_attn].T
        wz = w[:, 3 * d_attn:]
        qt, k, vt, g_out = _in_projection(
            x, mod, wqt, wk, wvt, wz,
            sgu_ln_g[l].reshape(1, -1), sgu_ln_b[l].reshape(1, -1),
            w_spatial[l], b_spatial[l].T)

        a = _diff_attention(lam, qt, k, vt, subln_g[l].reshape(-1, 1), 1.0 - lam_init)

        wo = w_out[l].astype(_BF16)
        x = _out_ffn(
            x, a, g_out, mod, wo[:d_attn], wo[d_attn:],
            ln1_g[l].reshape(1, d), ln1_b[l].reshape(1, d),
            w_gate[l].astype(_BF16), w_up[l].astype(_BF16), w_down[l].astype(_BF16),
            ln2_g[l].reshape(1, d), ln2_b[l].reshape(1, d), alpha)
    return x
```

```python
import functools
import math

import jax
import jax.numpy as jnp
from jax import lax
from jax.experimental import pallas as pl
from jax.experimental.pallas import tpu as pltpu

ATTN_HEADS = 4
ATTN_HEAD_DIM = 64
ATTN_V_DIM = 2 * ATTN_HEAD_DIM
SGU_GROUPS = 4
SGU_GROUP_DIM = 128
CHUNK = 128
LN_EPS = 1e-5
RMS_EPS = 1e-5

SUBLANES = 8
BF16_SUBLANES = 16
LANES = 128
V7X_VMEM_BYTES = 64 * 1024 * 1024

PROJ_ROWS = 1024
PROJ_SUBROWS = 512
ATTN_BLOCK = 512
ATTN_LEAD = 2
ATTN_UNROLL = 3
FFN_ROWS = 1024
FFN_SUBROWS = 256
FFN_COLS = 1536

_BF16 = jnp.bfloat16
_F32 = jnp.float32


def _dot(a, b):
    return jnp.dot(a, b, preferred_element_type=_F32)


def _dot_nt(a, b):
    return lax.dot_general(a, b, (((1,), (1,)), ((), ())), preferred_element_type=_F32)


def _nbytes(shape, dtype):
    return math.prod(shape) * jnp.dtype(dtype).itemsize


def _vmem_limit(pipelined, single, live):
    need = 2 * sum(pipelined) + sum(single) + live
    assert need <= V7X_VMEM_BYTES, need
    return need


def _layer_norm_rows(x, g, b):
    mu = jnp.mean(x, axis=-1, keepdims=True)
    xc = x - mu
    var = jnp.mean(xc * xc, axis=-1, keepdims=True)
    return xc * lax.rsqrt(var + LN_EPS) * g + b


def _mod_kernel(c_ref, w_ref, b_ref, lam_vec_ref, mod_ref, lam_ref, *, lam_init):
    c = c_ref[...]
    c_act = c * jax.nn.sigmoid(c)
    mod_ref[...] = _dot(c_act.astype(_BF16), w_ref[...].astype(_BF16)) + b_ref[...]
    lv = lam_vec_ref[...]
    s1 = jnp.sum(lv[0:1] * lv[1:2], axis=-1, keepdims=True)
    s2 = jnp.sum(lv[2:3] * lv[3:4], axis=-1, keepdims=True)
    lam = jnp.exp(s1) - jnp.exp(s2) + lam_init
    lam_ref[...] = jnp.broadcast_to(lam, lam_ref.shape)


def _adaln_mod(c, w_ada, b_ada, lam_vecs, lam_init):
    bsz, d = c.shape
    n = w_ada.shape[1]
    c_pad = jnp.zeros((SUBLANES, d), _F32).at[:bsz].set(c)
    mod, lam = pl.pallas_call(
        functools.partial(_mod_kernel, lam_init=lam_init),
        grid=(n // d,),
        in_specs=[
            pl.BlockSpec((SUBLANES, d), lambda j: (0, 0)),
            pl.BlockSpec((d, d), lambda j: (0, j)),
            pl.BlockSpec((1, d), lambda j: (0, j)),
            pl.BlockSpec(lam_vecs.shape, lambda j: (0, 0)),
        ],
        out_specs=[
            pl.BlockSpec((SUBLANES, d), lambda j: (0, j)),
            pl.BlockSpec((SUBLANES, LANES), lambda j: (0, 0)),
        ],
        out_shape=[
            jax.ShapeDtypeStruct((SUBLANES, n), _F32),
            jax.ShapeDtypeStruct((SUBLANES, LANES), _F32),
        ],
        name="adaln_mod",
    )(c_pad, w_ada, b_ada.reshape(1, n), lam_vecs)
    return mod[:bsz].reshape(bsz, n // d, d), lam[0, :1]


def _proj_kernel(x_ref, mod_ref, wqt_ref, wk_ref, wvt_ref, wz_ref, lng_ref, lnb_ref,
                 wsp_ref, bsp_ref, qt_ref, k_ref, vt_ref, g_ref, *, q_scale):
    rows = x_ref.shape[0]
    blk = vt_ref.shape[-1]
    mod = mod_ref[...]
    starts = list(range(0, rows, PROJ_SUBROWS))
    h = [(x_ref[r:r + PROJ_SUBROWS, :] * (1.0 + mod[1:2]) + mod[0:1]).astype(_BF16)
         for r in starts]
    z = [_dot(v, wz_ref[...]) for v in h]

    d_sgu = SGU_GROUPS * SGU_GROUP_DIM
    n_chunks = PROJ_SUBROWS // CHUNK
    t_idx = lax.broadcasted_iota(jnp.int32, (CHUNK, CHUNK), 0)
    s_idx = lax.broadcasted_iota(jnp.int32, (CHUNK, CHUNK), 1)
    causal = s_idx <= t_idx
    w_sp = [jnp.where(causal, wsp_ref[g], 0.0).astype(_BF16) for g in range(SGU_GROUPS)]
    for r, v, zr in zip(starts, h, z):
        k = _dot(v, wk_ref[...]).astype(_BF16)
        for head in range(ATTN_HEADS):
            k_ref[head, r:r + PROJ_SUBROWS, :] = k[:, head * ATTN_V_DIM:(head + 1) * ATTN_V_DIM]
        qt_ref[:, r:r + PROJ_SUBROWS] = (_dot_nt(wqt_ref[...], v) * q_scale).astype(_BF16)
        lane = r % blk
        vt_ref[r // blk, :, lane:lane + PROJ_SUBROWS] = _dot_nt(wvt_ref[...], v).astype(_BF16)
        zr = jax.nn.gelu(zr, approximate=True)
        for g in range(SGU_GROUPS):
            lo = g * SGU_GROUP_DIM
            hi = lo + SGU_GROUP_DIM
            u = zr[:, lo:hi]
            vs = zr[:, d_sgu + lo:d_sgu + hi]
            vn = _layer_norm_rows(vs, lng_ref[:, lo:hi], lnb_ref[:, lo:hi]).astype(_BF16)
            v_cat = jnp.concatenate(
                [vn[c * CHUNK:(c + 1) * CHUNK] for c in range(n_chunks)], axis=1)
            gate = _dot(w_sp[g], v_cat) + bsp_ref[:, g:g + 1]
            for c in range(n_chunks):
                g_ref[r + c * CHUNK:r + (c + 1) * CHUNK, lo:hi] = (
                    u[c * CHUNK:(c + 1) * CHUNK] * gate[:, c * CHUNK:(c + 1) * CHUNK]
                ).astype(_BF16)


def _in_projection(x, mod, wqt, wk, wvt, wz, ln_g, ln_b, w_sp, b_sp_t):
    bsz, seq, d = x.shape
    d_attn = wk.shape[1]
    d_z = wz.shape[1]
    rows = PROJ_ROWS
    blk = ATTN_BLOCK
    const = lambda *shape: pl.BlockSpec(shape, lambda b, i: (0,) * len(shape),
                                        pipeline_mode=pl.Buffered(1))
    vmem = _vmem_limit(
        pipelined=[_nbytes((rows, d), _F32), _nbytes(mod.shape[1:], _F32),
                   3 * _nbytes((rows, d_attn), _BF16), _nbytes((rows, d_z // 2), _BF16)],
        single=[3 * _nbytes((d, d_attn), _BF16), _nbytes((d, d_z), _BF16),
                _nbytes(w_sp.shape, _F32), _nbytes((CHUNK, LANES), _F32)],
        live=(_nbytes((rows, d_z), _F32) + _nbytes((rows, d), _BF16)
              + 2 * _nbytes((PROJ_SUBROWS, d_z), _F32)))
    return pl.pallas_call(
        functools.partial(_proj_kernel, q_scale=ATTN_HEAD_DIM ** -0.5 * math.log2(math.e)),
        grid=(bsz, seq // rows),
        in_specs=[
            pl.BlockSpec((None, rows, d), lambda b, i: (b, i, 0)),
            pl.BlockSpec((None,) + mod.shape[1:], lambda b, i: (b, 0, 0)),
            const(d_attn, d), const(d, d_attn), const(d_attn, d), const(d, d_z),
            const(1, d_z // 2), const(1, d_z // 2),
            const(SGU_GROUPS, CHUNK, CHUNK), const(CHUNK, SGU_GROUPS),
        ],
        out_specs=[
            pl.BlockSpec((None, d_attn, rows), lambda b, i: (b, 0, i)),
            pl.BlockSpec((None, ATTN_HEADS, rows, ATTN_V_DIM), lambda b, i: (b, 0, i, 0)),
            pl.BlockSpec((None, rows // blk, d_attn, blk), lambda b, i: (b, i, 0, 0)),
            pl.BlockSpec((None, rows, d_z // 2), lambda b, i: (b, i, 0)),
        ],
        out_shape=[
            jax.ShapeDtypeStruct((bsz, d_attn, seq), _BF16),
            jax.ShapeDtypeStruct((bsz, ATTN_HEADS, seq, ATTN_V_DIM), _BF16),
            jax.ShapeDtypeStruct((bsz, seq // blk, d_attn, blk), _BF16),
            jax.ShapeDtypeStruct((bsz, seq, d_z // 2), _BF16),
        ],
        compiler_params=pltpu.CompilerParams(
            dimension_semantics=("parallel", "parallel"),
            vmem_limit_bytes=vmem),
        name="in_projection",
    )(x, mod, wqt, wk, wvt, wz, ln_g, ln_b, w_sp, b_sp_t)


def _attn_kernel(lam_ref, qt_ref, k_ref, vt_ref, g_ref, o_ref,
                 qz_ref, s_ref, cmax_ref, m_ref, acc_ref, *, out_scale):
    blk = qt_ref.shape[-1]
    e_dim = ATTN_V_DIM
    i = pl.program_id(1)
    heads = range(ATTN_HEADS)
    cols = [slice(h * e_dim, (h + 1) * e_dim) for h in heads]

    for h in heads:
        qt = qt_ref[cols[h], :]
        row = lax.broadcasted_iota(jnp.int32, qt.shape, 0)
        zero = jnp.zeros_like(qt)
        qz_ref[h, :, :blk] = jnp.where(row < ATTN_HEAD_DIM, qt, zero)
        qz_ref[h, :, blk:] = jnp.where(row >= ATTN_HEAD_DIM, qt, zero)

    m_ref[...] = jnp.full(m_ref.shape, -jnp.inf, _F32)
    acc_ref[...] = jnp.zeros(acc_ref.shape, _F32)
    ones_rows = jnp.ones((acc_ref.shape[1] - e_dim, blk), _BF16)

    half = blk // 2
    late = [slice(half, blk), slice(blk + half, 2 * blk)]

    def causal(s, lane_offset):
        key = lax.broadcasted_iota(jnp.int32, s.shape, 0)
        qry = lax.broadcasted_iota(jnp.int32, s.shape, 1) % half + lane_offset
        return jnp.where(key <= qry, s, -jnp.inf)

    def produce(h, kb, masked):
        start = pl.multiple_of(kb * blk, blk)
        if not masked:
            s = _dot(k_ref[h, pl.ds(start, blk), :], qz_ref[h])
            s_ref[h] = s
            cmax_ref[h] = jnp.max(s, axis=0, keepdims=True)
            return
        early = _dot(k_ref[h, pl.ds(start, half), :], qz_ref[h])
        lane = lax.broadcasted_iota(jnp.int32, early.shape, 1) % blk
        early = jnp.where(lane >= half, early, causal(early, 0))
        s_ref[h, :half, :] = early
        cmax = jnp.max(early, axis=0, keepdims=True)
        q_late = jnp.concatenate([qz_ref[h, :, c] for c in late], axis=1)
        late_s = causal(_dot(k_ref[h, pl.ds(start + half, half), :], q_late), 0)
        for n, c in enumerate(late):
            piece = late_s[:, n * half:(n + 1) * half]
            s_ref[h, half:, c] = piece
            cmax_ref[h, :, c] = jnp.maximum(cmax[:, c], jnp.max(piece, axis=0, keepdims=True))
        for c in (slice(0, half), slice(blk, blk + half)):
            cmax_ref[h, :, c] = cmax[:, c]

    def consume(h, kb, masked):
        m_prev = m_ref[h]
        m_new = jnp.maximum(m_prev, cmax_ref[h])
        alpha = jnp.exp2(m_prev - m_new)
        m_ref[h] = m_new
        vt_ones = jnp.concatenate([vt_ref[kb, cols[h], :], ones_rows], axis=0)
        if not masked:
            e = jnp.exp2(s_ref[h] - m_new).astype(_BF16)
            acc_ref[h] = alpha * acc_ref[h] + _dot(vt_ones, e)
            return
        e = jnp.exp2(s_ref[h, :half, :] - m_new).astype(_BF16)
        pv = _dot(vt_ones[:, :half], e)
        for c in (slice(0, half), slice(blk, blk + half)):
            acc_ref[h, :, c] = alpha[:, c] * acc_ref[h, :, c] + pv[:, c]
        for c in late:
            e = jnp.exp2(s_ref[h, half:, c] - m_new[:, c]).astype(_BF16)
            acc_ref[h, :, c] = (alpha[:, c] * acc_ref[h, :, c] + pv[:, c]
                                + _dot(vt_ones[:, half:], e))

    def block_steps(kb, masked, next_masked):
        for h in heads:
            nxt = h + ATTN_LEAD
            if nxt < ATTN_HEADS:
                produce(nxt, kb, masked)
            elif next_masked is not None:
                produce(nxt - ATTN_HEADS, kb + 1, next_masked)
            consume(h, kb, masked)

    @pl.when(i > 0)
    def _():
        for h in range(ATTN_LEAD):
            produce(h, 0, False)

    n_plain = jnp.maximum(i - 1, 0)
    n_steps = n_plain // ATTN_UNROLL
    n_left = n_plain - n_steps * ATTN_UNROLL

    def step(u, carry):
        for j in range(ATTN_UNROLL):
            block_steps(ATTN_UNROLL * u + j, False, False)
        return carry

    lax.fori_loop(0, n_steps, step, 0)

    @pl.when(i == 0)
    def _():
        for h in range(ATTN_LEAD):
            produce(h, 0, True)
        block_steps(i, True, None)

    for left in range(ATTN_UNROLL):
        @pl.when(jnp.logical_and(i > 0, n_left == left))
        def _():
            for j in range(left, 0, -1):
                block_steps(i - 1 - j, False, False)
            block_steps(i - 1, False, True)
            block_steps(i, True, None)

    lam = lam_ref[0]
    for h in heads:
        acc = acc_ref[h]
        o = acc[:e_dim] / acc[e_dim:e_dim + 1]
        o = o[:, :blk] - lam * o[:, blk:]
        ms = jnp.mean(o * o, axis=0, keepdims=True)
        y = o * lax.rsqrt(ms + RMS_EPS) * g_ref[...] * out_scale
        o_ref[:, cols[h]] = y.T.astype(o_ref.dtype)


def _diff_attention(lam, qt, k, vt, subln_g_col, out_scale):
    bsz, d_attn, seq = qt.shape
    blk = ATTN_BLOCK
    e = ATTN_V_DIM
    scratch = [
        ((ATTN_HEADS, e, 2 * blk), _BF16),
        ((ATTN_HEADS, blk, 2 * blk), _F32),
        ((ATTN_HEADS, 1, 2 * blk), _F32),
        ((ATTN_HEADS, 1, 2 * blk), _F32),
        ((ATTN_HEADS, e + BF16_SUBLANES, 2 * blk), _F32),
    ]
    vmem = _vmem_limit(
        pipelined=[2 * _nbytes((blk, d_attn), _BF16), 2 * _nbytes((seq, d_attn), _BF16),
                   _nbytes((e, LANES), _F32)],
        single=[_nbytes(*s) for s in scratch],
        live=2 * (_nbytes((blk, 2 * blk), _F32) + _nbytes((blk, 2 * blk), _BF16)))
    return pl.pallas_call(
        functools.partial(_attn_kernel, out_scale=out_scale),
        grid=(bsz, seq // blk),
        in_specs=[
            pl.BlockSpec(memory_space=pltpu.SMEM),
            pl.BlockSpec((None, d_attn, blk), lambda b, i: (b, 0, i)),
            pl.BlockSpec((None, ATTN_HEADS, seq, e), lambda b, i: (b, 0, 0, 0)),
            pl.BlockSpec((None, seq // blk, d_attn, blk), lambda b, i: (b, 0, 0, 0)),
            pl.BlockSpec((e, 1), lambda b, i: (0, 0)),
        ],
        out_specs=pl.BlockSpec((None, blk, d_attn), lambda b, i: (b, i, 0)),
        out_shape=jax.ShapeDtypeStruct((bsz, seq, d_attn), _BF16),
        scratch_shapes=[pltpu.VMEM(*s) for s in scratch],
        compiler_params=pltpu.CompilerParams(
            dimension_semantics=("parallel", "arbitrary"),
            vmem_limit_bytes=vmem),
        name="diff_attention",
    )(lam, qt, k, vt, subln_g_col)


def _ffn_kernel(x_ref, a_ref, g_ref, mod_ref, wo_a_ref, wo_g_ref, ln1g_ref, ln1b_ref,
                wgate_ref, wup_ref, wdown_ref, ln2g_ref, ln2b_ref, o_ref, *, alpha):
    mod = mod_ref[...]
    rows = x_ref.shape[0]
    groups = [slice(r, r + FFN_SUBROWS) for r in range(0, rows, FFN_SUBROWS)]
    mixed = [_dot(a_ref[r, :], wo_a_ref[...]) + _dot(g_ref[r, :], wo_g_ref[...]) for r in groups]
    x1 = [_layer_norm_rows(alpha * x_ref[r, :] + (1.0 + mod[2:3]) * mx,
                           ln1g_ref[...], ln1b_ref[...]) for r, mx in zip(groups, mixed)]
    h = [(v * (1.0 + mod[4:5]) + mod[3:4]).astype(_BF16) for v in x1]
    d_ff = wgate_ref.shape[1]
    f = [None] * len(groups)
    for lo in range(0, d_ff, FFN_COLS):
        cols = slice(lo, min(lo + FFN_COLS, d_ff))
        gate = [_dot(v, wgate_ref[:, cols]) for v in h]
        up = [_dot(v, wup_ref[:, cols]) for v in h]
        act = [(gt * jax.nn.sigmoid(gt) * u).astype(_BF16) for gt, u in zip(gate, up)]
        part = [_dot(v, wdown_ref[cols, :]) for v in act]
        f = [p if acc is None else acc + p for acc, p in zip(f, part)]
    for r, v, fv in zip(groups, x1, f):
        o_ref[r, :] = _layer_norm_rows(alpha * v + (1.0 + mod[5:6]) * fv,
                                       ln2g_ref[...], ln2b_ref[...])


def _out_ffn(x, a, g_out, mod, wo_a, wo_g, ln1_g, ln1_b, w_gate, w_up, w_down, ln2_g, ln2_b,
             alpha):
    bsz, seq, d = x.shape
    rows = FFN_ROWS
    d_half = a.shape[-1]
    d_ff = w_gate.shape[1]
    const = lambda *shape: pl.BlockSpec(shape, lambda b, i: (0,) * len(shape),
                                        pipeline_mode=pl.Buffered(1))
    vmem = _vmem_limit(
        pipelined=[2 * _nbytes((rows, d), _F32), 2 * _nbytes((rows, d_half), _BF16),
                   _nbytes(mod.shape[1:], _F32)],
        single=[2 * _nbytes((d_half, d), _BF16), 3 * _nbytes((d, d_ff), _BF16),
                4 * _nbytes((SUBLANES, d), _F32)],
        live=(2 * _nbytes((rows, d), _F32) + _nbytes((rows, d), _BF16)
              + 2 * (2 * _nbytes((FFN_SUBROWS, FFN_COLS), _F32)
                     + _nbytes((FFN_SUBROWS, FFN_COLS), _BF16))))
    return pl.pallas_call(
        functools.partial(_ffn_kernel, alpha=alpha),
        grid=(bsz, seq // rows),
        in_specs=[
            pl.BlockSpec((None, rows, d), lambda b, i: (b, i, 0)),
            pl.BlockSpec((None, rows, d_half), lambda b, i: (b, i, 0)),
            pl.BlockSpec((None, rows, d_half), lambda b, i: (b, i, 0)),
            pl.BlockSpec((None,) + mod.shape[1:], lambda b, i: (b, 0, 0)),
            const(d_half, d), const(d_half, d), const(1, d), const(1, d),
            const(d, d_ff), const(d, d_ff), const(d_ff, d), const(1, d), const(1, d),
        ],
        out_specs=pl.BlockSpec((None, rows, d), lambda b, i: (b, i, 0)),
        out_shape=jax.ShapeDtypeStruct((bsz, seq, d), x.dtype),
        compiler_params=pltpu.CompilerParams(
            dimension_semantics=("parallel", "parallel"),
            vmem_limit_bytes=vmem),
        name="out_ffn",
    )(x, a, g_out, mod, wo_a, wo_g, ln1_g, ln1_b, w_gate, w_up, w_down, ln2_g, ln2_b)


def kernel(x, c, w_ada, b_ada, w_in, lambda_q1, lambda_k1, lambda_q2, lambda_k2, subln_g,
           sgu_ln_g, sgu_ln_b, w_spatial, b_spatial, w_out, ln1_g, ln1_b, w_gate, w_up, w_down,
           ln2_g, ln2_b):
    depth = w_in.shape[0]
    d = x.shape[-1]
    d_attn = ATTN_HEADS * ATTN_V_DIM
    alpha = (2 * depth) ** 0.25
    for l in range(depth):
        lam_init = 0.8 - 0.6 * math.exp(-0.3 * l)
        lam_vecs = jnp.stack([lambda_q1[l], lambda_k1[l], lambda_q2[l], lambda_k2[l]])
        mod, lam = _adaln_mod(c, w_ada[l], b_ada[l], lam_vecs.astype(_F32), lam_init)

        w = w_in[l].astype(_BF16)
        wqt = w[:, :d_attn].T
        wk = w[:, d_attn:2 * d_attn]
        wvt = w[:, 2 * d_attn:3 * d_attn].T
        wz = w[:, 3 * d_attn:]
        qt, k, vt, g_out = _in_projection(
            x, mod, wqt, wk, wvt, wz,
            sgu_ln_g[l].reshape(1, -1), sgu_ln_b[l].reshape(1, -1),
            w_spatial[l], b_spatial[l].T)

        a = _diff_attention(lam, qt, k, vt, subln_g[l].reshape(-1, 1), 1.0 - lam_init)

        wo = w_out[l].astype(_BF16)
        x = _out_ffn(
            x, a, g_out, mod, wo[:d_attn], wo[d_attn:],
            ln1_g[l].reshape(1, d), ln1_b[l].reshape(1, d),
            w_gate[l].astype(_BF16), w_up[l].astype(_BF16), w_down[l].astype(_BF16),
            ln2_g[l].reshape(1, d), ln2_b[l].reshape(1, d), alpha)
    return x
```
